```python
import math
import jax, jax.numpy as jnp
from jax import lax
import numpy as np

D_MODEL = 4096
BATCH = 2
SEQ = 8192
DEPTH = 1
DEC_BATCH = 4
DEC_SEQ = 2048
PAST_LEN = 128

W_SSM = D_MODEL // 2
SSM_GROUP_CH = 16
SSM_GROUPS = W_SSM // SSM_GROUP_CH
SSM_STATE = 64
W_CONV = D_MODEL // 2
CONV_WIDTH = 31
CONV_HALF = CONV_WIDTH // 2
W_IN = W_SSM + 2 * W_CONV
N_EXPERTS = 128
TOP_K = 8
D_EXPERT = D_MODEL // 8
D_SHARED = D_EXPERT
N_ROUTE_GROUPS = 8
TOPK_ROUTE_GROUPS = 4
ROUTED_SCALE = 2.5
ROW_BLOCK = 128
RMS_EPS = 1e-6
LN_EPS = 1e-5

kernel_name = "hybrid_s5_conformer_moe_encoder"


def rmsnorm(x, g):
    xf = x.astype(jnp.float32)
    y = xf * lax.rsqrt(jnp.mean(xf * xf, axis=-1, keepdims=True) + RMS_EPS)
    return (y * g.astype(jnp.float32)).astype(x.dtype)


def layernorm(x, g, b):
    xf = x.astype(jnp.float32)
    mu = jnp.mean(xf, axis=-1, keepdims=True)
    xc = xf - mu
    var = jnp.mean(xc * xc, axis=-1, keepdims=True)
    y = xc * lax.rsqrt(var + LN_EPS) * g.astype(jnp.float32) + b.astype(jnp.float32)
    return y.astype(x.dtype)


def _ssm_combine(e_i, e_j):
    a_i, b_i = e_i
    a_j, b_j = e_j
    return a_j * a_i, a_j * b_i + b_j


def s5_bidirectional(u, a_re, a_im, log_dt, b_re, b_im, c_re, c_im, d):
    bsz, seq, _ = u.shape
    uf = u.astype(jnp.float32).reshape(bsz, seq, SSM_GROUPS, SSM_GROUP_CH)
    h_sum = None
    for direction in range(2):
        lam = lax.complex(a_re[direction].astype(jnp.float32), a_im[direction].astype(jnp.float32))
        dt = jnp.exp(log_dt[direction].astype(jnp.float32))[:, None]
        lam_bar = jnp.exp(lam * dt)
        b = lax.complex(b_re[direction].astype(jnp.float32), b_im[direction].astype(jnp.float32))
        b_bar = ((lam_bar - 1.0) / lam)[..., None] * b
        bu = jnp.einsum('blgc,gnc->blgn', uf, b_bar)
        a_seq = jnp.broadcast_to(lam_bar, (1, seq) + lam_bar.shape)
        _, h = lax.associative_scan(_ssm_combine, (a_seq, bu), reverse=(direction == 1), axis=1)
        h_sum = h if h_sum is None else h_sum + h
    y = (jnp.einsum('blgn,gcn->blgc', h_sum.real, c_re.astype(jnp.float32))
         - jnp.einsum('blgn,gcn->blgc', h_sum.imag, c_im.astype(jnp.float32)))
    y = y.reshape(bsz, seq, W_SSM) + d.astype(jnp.float32) * uf.reshape(bsz, seq, W_SSM)
    return y.astype(u.dtype)


def depthwise_conv(u, w, b):
    out = lax.conv_general_dilated(
        u, w[:, None, :], window_strides=(1,), padding=[(CONV_HALF, CONV_HALF)],
        dimension_numbers=('NWC', 'WIO', 'NWC'), feature_group_count=u.shape[-1])
    return out + b


def moe_routed(h, router_w, router_bias, w_g, w_u, w_d):
    T, D = h.shape
    scores = jax.nn.sigmoid(h.astype(jnp.float32) @ router_w.astype(jnp.float32))
    choice = scores + router_bias.astype(jnp.float32)
    grp = choice.reshape(T, N_ROUTE_GROUPS, N_EXPERTS // N_ROUTE_GROUPS)
    grp_score = jnp.sum(lax.top_k(grp, 2)[0], axis=-1)
    _, grp_idx = lax.top_k(grp_score, TOPK_ROUTE_GROUPS)
    grp_mask = jnp.sum(jax.nn.one_hot(grp_idx, N_ROUTE_GROUPS, dtype=jnp.float32), axis=1) > 0
    exp_mask = jnp.repeat(grp_mask, N_EXPERTS // N_ROUTE_GROUPS, axis=1)
    _, top_idx = lax.top_k(jnp.where(exp_mask, choice, -jnp.inf), TOP_K)
    top_s = jnp.take_along_axis(scores, top_idx, axis=1)
    weights = top_s / jnp.sum(top_s, axis=-1, keepdims=True) * ROUTED_SCALE

    A = T * TOP_K
    flat_e = top_idx.reshape(A).astype(jnp.int32)
    flat_tok = jnp.repeat(jnp.arange(T, dtype=jnp.int32), TOP_K)
    flat_w = weights.reshape(A)
    order = jnp.argsort(flat_e)
    e_sorted = flat_e[order]
    counts = jnp.bincount(flat_e, length=N_EXPERTS).astype(jnp.int32)
    padded = (counts + ROW_BLOCK - 1) // ROW_BLOCK * ROW_BLOCK
    pad_end = jnp.cumsum(padded)
    pad_start = pad_end - padded
    start = jnp.cumsum(counts) - counts
    dest = pad_start[e_sorted] + (jnp.arange(A, dtype=jnp.int32) - start[e_sorted])
    n_blocks = -(-A // ROW_BLOCK) + N_EXPERTS
    P = n_blocks * ROW_BLOCK
    row_tok = jnp.full((P,), T, jnp.int32).at[dest].set(flat_tok[order])
    row_w = jnp.zeros((P,), jnp.float32).at[dest].set(flat_w[order])
    blk_start = jnp.arange(n_blocks, dtype=jnp.int32) * ROW_BLOCK
    blk_e = jnp.minimum(jnp.searchsorted(pad_end, blk_start, side='right'), N_EXPERTS - 1).astype(jnp.int32)

    x_pad = jnp.concatenate([h, jnp.zeros((1, D), h.dtype)], axis=0)

    def body(y, blk):
        tok, wts, e = blk
        xb = x_pad[tok]
        act = jax.nn.silu(xb @ w_g[e]) * (xb @ w_u[e])
        out = act @ w_d[e]
        return y.at[tok].add(out * wts[:, None].astype(out.dtype)), None

    y0 = jnp.zeros((T + 1, D), h.dtype)
    y, _ = lax.scan(body, y0, (row_tok.reshape(n_blocks, ROW_BLOCK),
                               row_w.reshape(n_blocks, ROW_BLOCK), blk_e))
    return y[:T]


def encoder_layer(x, c, w_ada, b_ada, norm_mix, w_in, ssm_a_re, ssm_a_im, ssm_log_dt,
                  ssm_b_re, ssm_b_im, ssm_c_re, ssm_c_im, ssm_d, w_ssm_glu, w_branch_ssm,
                  conv_w, conv_b, conv_norm_g, conv_norm_b, w_branch_conv, w_merge_gate,
                  b_merge_gate, w_out, norm_ffn, router_w, router_bias, exp_w_gate, exp_w_up,
                  exp_w_down, shared_w_gate, shared_w_up, shared_w_down):
    bsz, seq, D = x.shape
    ada = jax.nn.silu(c) @ w_ada + b_ada
    sh1, sc1, g1, sh2, sc2, g2 = [t[:, None, :] for t in jnp.split(ada, 6, axis=-1)]

    h = rmsnorm(x, norm_mix) * (1 + sc1) + sh1
    proj = h @ w_in
    u_ssm = proj[..., :W_SSM]
    u_conv = proj[..., W_SSM:]

    ya = jax.nn.gelu(s5_bidirectional(u_ssm, ssm_a_re, ssm_a_im, ssm_log_dt, ssm_b_re, ssm_b_im,
                                      ssm_c_re, ssm_c_im, ssm_d))
    ga = ya @ w_ssm_glu
    ya = (ga[..., :W_SSM] * jax.nn.sigmoid(ga[..., W_SSM:])) @ w_branch_ssm

    ub = u_conv[..., :W_CONV] * jax.nn.sigmoid(u_conv[..., W_CONV:])
    ub = depthwise_conv(ub, conv_w, conv_b)
    ub = jax.nn.silu(layernorm(ub, conv_norm_g, conv_norm_b))
    yb = ub @ w_branch_conv

    gates = jax.nn.sigmoid(h @ w_merge_gate + b_merge_gate)
    mixed = (gates[..., :D] * ya + gates[..., D:] * yb) @ w_out
    x = x + g1 * mixed

    h2 = rmsnorm(x, norm_ffn) * (1 + sc2) + sh2
    hf = h2.reshape(bsz * seq, D)
    shared = (jax.nn.silu(hf @ shared_w_gate) * (hf @ shared_w_up)) @ shared_w_down
    routed = moe_routed(hf, router_w, router_bias, exp_w_gate, exp_w_up, exp_w_down)
    x = x + g2 * (shared + routed).reshape(bsz, seq, D)
    return x


def setup_inputs(seed: int = 0) -> dict:
    key = jax.random.key(seed)
    ks = jax.random.split(key, 40)
    f32 = jnp.float32
    L, D = DEPTH, D_MODEL

    def nrm(k, shape, scale):
        return jax.random.normal(k, shape, f32) * scale

    n_idx = jnp.arange(SSM_STATE, dtype=f32)
    ssm_shape = (L, 2, SSM_GROUPS, SSM_STATE)
    return {
        "x_prompt": nrm(ks[0], (BATCH, SEQ, D), 1.0),
        "x_sample": nrm(ks[1], (DEC_BATCH, DEC_SEQ, D), 1.0),
        "c_prompt": nrm(ks[2], (BATCH, D), 1.0),
        "c_sample": nrm(ks[3], (DEC_BATCH, D), 1.0),
        "w_ada": nrm(ks[4], (L, D, 6 * D), 0.5 * D ** -0.5),
        "b_ada": nrm(ks[5], (L, 6 * D), 0.02),
        "norm_mix": 1.0 + nrm(ks[6], (L, D), 0.02),
        "w_in": nrm(ks[7], (L, D, W_IN), D ** -0.5),
        "ssm_a_re": -0.5 + nrm(ks[8], ssm_shape, 0.01),
        "ssm_a_im": math.pi * n_idx + nrm(ks[9], ssm_shape, 0.01),
        "ssm_log_dt": jax.random.uniform(ks[10], (L, 2, SSM_GROUPS), f32, math.log(1e-3), math.log(1e-1)),
        "ssm_b_re": nrm(ks[11], ssm_shape + (SSM_GROUP_CH,), (2 * SSM_GROUP_CH) ** -0.5),
        "ssm_b_im": nrm(ks[12], ssm_shape + (SSM_GROUP_CH,), (2 * SSM_GROUP_CH) ** -0.5),
        "ssm_c_re": nrm(ks[13], (L, SSM_GROUPS, SSM_GROUP_CH, SSM_STATE), SSM_STATE ** -0.5),
        "ssm_c_im": nrm(ks[14], (L, SSM_GROUPS, SSM_GROUP_CH, SSM_STATE), SSM_STATE ** -0.5),
        "ssm_d": nrm(ks[15], (L, W_SSM), 1.0),
        "w_ssm_glu": nrm(ks[16], (L, W_SSM, 2 * W_SSM), W_SSM ** -0.5),
        "w_branch_ssm": nrm(ks[17], (L, W_SSM, D), W_SSM ** -0.5),
        "conv_w": nrm(ks[18], (L, CONV_WIDTH, W_CONV), CONV_WIDTH ** -0.5),
        "conv_b": nrm(ks[19], (L, W_CONV), 0.02),
        "conv_norm_g": 1.0 + nrm(ks[20], (L, W_CONV), 0.02),
        "conv_norm_b": nrm(ks[21], (L, W_CONV), 0.02),
        "w_branch_conv": nrm(ks[22], (L, W_CONV, D), W_CONV ** -0.5),
        "w_merge_gate": nrm(ks[23], (L, D, 2 * D), D ** -0.5),
        "b_merge_gate": nrm(ks[24], (L, 2 * D), 0.02),
        "w_out": nrm(ks[25], (L, D, D), D ** -0.5),
        "norm_ffn": 1.0 + nrm(ks[26], (L, D), 0.02),
        "router_w": nrm(ks[27], (L, D, N_EXPERTS), D ** -0.5),
        "router_bias": nrm(ks[28], (L, N_EXPERTS), 0.01),
        "exp_w_gate": nrm(ks[29], (L, N_EXPERTS, D, D_EXPERT), D ** -0.5),
        "exp_w_up": nrm(ks[30], (L, N_EXPERTS, D, D_EXPERT), D ** -0.5),
        "exp_w_down": nrm(ks[31], (L, N_EXPERTS, D_EXPERT, D), D_EXPERT ** -0.5),
        "shared_w_gate": nrm(ks[32], (L, D, D_SHARED), D ** -0.5),
        "shared_w_up": nrm(ks[33], (L, D, D_SHARED), D ** -0.5),
        "shared_w_down": nrm(ks[34], (L, D_SHARED, D), D_SHARED ** -0.5),
        "final_norm": 1.0 + nrm(ks[35], (D,), 0.02),
    }


def reference(x_prompt, x_sample, c_prompt, c_sample, w_ada, b_ada, norm_mix, w_in,
              ssm_a_re, ssm_a_im, ssm_log_dt, ssm_b_re, ssm_b_im, ssm_c_re, ssm_c_im, ssm_d,
              w_ssm_glu, w_branch_ssm, conv_w, conv_b, conv_norm_g, conv_norm_b, w_branch_conv,
              w_merge_gate, b_merge_gate, w_out, norm_ffn, router_w, router_bias,
              exp_w_gate, exp_w_up, exp_w_down, shared_w_gate, shared_w_up, shared_w_down,
              final_norm):
    def trunk(x, c):
        for l in range(DEPTH):
            x = encoder_layer(
                x, c, w_ada[l], b_ada[l], norm_mix[l], w_in[l], ssm_a_re[l], ssm_a_im[l],
                ssm_log_dt[l], ssm_b_re[l], ssm_b_im[l], ssm_c_re[l], ssm_c_im[l], ssm_d[l],
                w_ssm_glu[l], w_branch_ssm[l], conv_w[l], conv_b[l], conv_norm_g[l], conv_norm_b[l],
                w_branch_conv[l], w_merge_gate[l], b_merge_gate[l], w_out[l], norm_ffn[l],
                router_w[l], router_bias[l], exp_w_gate[l], exp_w_up[l], exp_w_down[l],
                shared_w_gate[l], shared_w_up[l], shared_w_down[l])
        return rmsnorm(x, final_norm)

    y_prompt = trunk(x_prompt, c_prompt)
    y_sample = trunk(x_sample, c_sample)
    return (y_prompt, y_sample)
```

```python
import functools
import math

import jax
import jax.numpy as jnp
import numpy as np
from jax import lax
from jax.experimental import pallas as pl
from jax.experimental.pallas import tpu as pltpu

F32 = jnp.float32
BF16 = jnp.bfloat16
I32 = jnp.int32
U32 = jnp.uint32

SSM_GROUP_CH = 16
SSM_CHUNK = 16
N_ROUTE_GROUPS = 8
TOPK_ROUTE_GROUPS = 4
TOP_K = 8
ROUTED_SCALE = 2.5
RMS_EPS = 1e-6
LN_EPS = 1e-5

V7X_VMEM_LIMIT = 56 * 1024 * 1024
LANES = 128


def _cparams(sem, vmem=V7X_VMEM_LIMIT):
    return pltpu.CompilerParams(dimension_semantics=sem, vmem_limit_bytes=vmem)


def _tile(n, want):
    t = min(n, want)
    while n % t:
        t -= 1
    return t


def _sigmoid(x):
    return 1.0 / (1.0 + jnp.exp(-x))


def _bdot(a, b):
    return jnp.dot(a, b, preferred_element_type=F32)


def _ada_kernel(c_ref, w_ref, b_ref, o_ref):
    c = c_ref[...]
    s = (c * _sigmoid(c)).astype(BF16)
    o_ref[...] = _bdot(s, w_ref[...].astype(BF16)) + b_ref[...]


def _ada(c_all, w_ada, b_ada):
    nseq, d = c_all.shape
    n = w_ada.shape[1]
    tn = _tile(n, 512)
    return pl.pallas_call(
        _ada_kernel,
        out_shape=jax.ShapeDtypeStruct((nseq, n), F32),
        grid=(n // tn,),
        in_specs=[pl.BlockSpec((nseq, d), lambda j: (0, 0)),
                  pl.BlockSpec((d, tn), lambda j: (0, j)),
                  pl.BlockSpec((1, tn), lambda j: (0, j))],
        out_specs=pl.BlockSpec((nseq, tn), lambda j: (0, j)),
        compiler_params=_cparams(("arbitrary",)),
        name="ada",
    )(c_all, w_ada, b_ada.reshape(1, n))


def _pack_bf16_pair(lo, hi):
    lo_b = pltpu.bitcast(lo.astype(BF16).astype(F32), U32) >> 16
    hi_b = pltpu.bitcast(hi.astype(BF16).astype(F32), U32) & jnp.uint32(0xFFFF0000)
    return hi_b | lo_b


def _unpack_bf16_pair(w):
    lo = pltpu.bitcast(w << 16, F32)
    hi = pltpu.bitcast(w & jnp.uint32(0xFFFF0000), F32)
    return lo, hi


def _modulated_norm(x, g, sc, sh):
    ms = jnp.mean(x * x, axis=-1, keepdims=True)
    return x * lax.rsqrt(ms + RMS_EPS) * g * (1.0 + sc) + sh


def _norm1_kernel(seq_ref, x_ref, g_ref, sc_ref, sh_ref, o_ref):
    del seq_ref
    o_ref[...] = _modulated_norm(x_ref[...], g_ref[...], sc_ref[0], sh_ref[0]).astype(BF16)


def _norm2_kernel(seq_ref, x_ref, g_ref, sc_ref, sh_ref, rw_ref, h_ref, hp_ref, lg_ref):
    del seq_ref
    h = _modulated_norm(x_ref[...], g_ref[...], sc_ref[0], sh_ref[0])
    half = h.shape[1] // 2
    h_ref[...] = h.astype(BF16)
    hp_ref[...] = _pack_bf16_pair(h[:, :half], h[:, half:])
    lg_ref[...] = jnp.dot(h, rw_ref[...], preferred_element_type=F32,
                          precision=lax.Precision.HIGHEST)


def _norm1(x, g, sc, sh, seq_of_tile, tm):
    t, d = x.shape
    row = lambda i, s: (i, 0)
    per_seq = lambda i, s: (s[i], 0, 0)
    return pl.pallas_call(
        _norm1_kernel,
        out_shape=jax.ShapeDtypeStruct((t, d), BF16),
        grid_spec=pltpu.PrefetchScalarGridSpec(
            num_scalar_prefetch=1, grid=(t // tm,),
            in_specs=[pl.BlockSpec((tm, d), row),
                      pl.BlockSpec((1, d), lambda i, s: (0, 0)),
                      pl.BlockSpec((1, 1, d), per_seq),
                      pl.BlockSpec((1, 1, d), per_seq)],
            out_specs=pl.BlockSpec((tm, d), row)),
        compiler_params=_cparams(("arbitrary",)),
        name="norm1",
    )(seq_of_tile, x, g.reshape(1, d), sc, sh)


def _norm2(x, g, sc, sh, router_w, seq_of_tile, tm):
    t, d = x.shape
    e = router_w.shape[1]
    row = lambda i, s: (i, 0)
    per_seq = lambda i, s: (s[i], 0, 0)
    return pl.pallas_call(
        _norm2_kernel,
        out_shape=(jax.ShapeDtypeStruct((t, d), BF16),
                   jax.ShapeDtypeStruct((t, d // 2), U32),
                   jax.ShapeDtypeStruct((t, e), F32)),
        grid_spec=pltpu.PrefetchScalarGridSpec(
            num_scalar_prefetch=1, grid=(t // tm,),
            in_specs=[pl.BlockSpec((tm, d), row),
                      pl.BlockSpec((1, d), lambda i, s: (0, 0)),
                      pl.BlockSpec((1, 1, d), per_seq),
                      pl.BlockSpec((1, 1, d), per_seq),
                      pl.BlockSpec((d, e), lambda i, s: (0, 0))],
            out_specs=(pl.BlockSpec((tm, d), row),
                       pl.BlockSpec((tm, d // 2), row),
                       pl.BlockSpec((tm, e), row))),
        compiler_params=_cparams(("arbitrary",)),
        name="norm2_router",
    )(seq_of_tile, x, g.reshape(1, d), sc, sh, router_w)


def _proj_kernel(x_ref, w_ref, o_ref):
    o_ref[...] = _bdot(x_ref[...], w_ref[...]).astype(o_ref.dtype)


def _proj(x, w, tm, tn, out_dtype=BF16, name="proj"):
    t, k = x.shape
    n = w.shape[1]
    return pl.pallas_call(
        _proj_kernel,
        out_shape=jax.ShapeDtypeStruct((t, n), out_dtype),
        grid=(t // tm, n // tn),
        in_specs=[pl.BlockSpec((tm, k), lambda i, j: (i, 0)),
                  pl.BlockSpec((k, tn), lambda i, j: (0, j))],
        out_specs=pl.BlockSpec((tm, tn), lambda i, j: (i, j)),
        compiler_params=_cparams(("parallel", "arbitrary")),
        name=name,
    )(x, w)


def _glu_kernel(x_ref, wa_ref, wb_ref, o_ref, *, silu_first):
    x = x_ref[...]
    a = _bdot(x, wa_ref[...])
    b = _bdot(x, wb_ref[...])
    if silu_first:
        y = a * _sigmoid(a) * b
    else:
        y = a * _sigmoid(b)
    o_ref[...] = y.astype(o_ref.dtype)


def _glu_proj(x, wa, wb, tm, tn, silu_first, name):
    t, k = x.shape
    n = wa.shape[1]
    return pl.pallas_call(
        functools.partial(_glu_kernel, silu_first=silu_first),
        out_shape=jax.ShapeDtypeStruct((t, n), BF16),
        grid=(t // tm, n // tn),
        in_specs=[pl.BlockSpec((tm, k), lambda i, j: (i, 0)),
                  pl.BlockSpec((k, tn), lambda i, j: (0, j)),
                  pl.BlockSpec((k, tn), lambda i, j: (0, j))],
        out_specs=pl.BlockSpec((tm, tn), lambda i, j: (i, j)),
        compiler_params=_cparams(("parallel", "arbitrary")),
        name=name,
    )(x, wa, wb)


def _merge_kernel(h_ref, a_ref, c_ref, wga_ref, wgb_ref, wa_ref, wc_ref, ba_ref, bb_ref, o_ref):
    h = h_ref[...]
    ga = _sigmoid(_bdot(h, wga_ref[...]) + ba_ref[...])
    gb = _sigmoid(_bdot(h, wgb_ref[...]) + bb_ref[...])
    ya = _bdot(a_ref[...], wa_ref[...])
    yb = _bdot(c_ref[...], wc_ref[...])
    o_ref[...] = (ga * ya + gb * yb).astype(o_ref.dtype)


def _merge(h, a, cv, wga, wgb, wa, wc, ba, bb, tm, tn):
    t, d = h.shape
    ka = a.shape[1]
    kc = cv.shape[1]
    row = lambda i, j: (i, 0)
    col = lambda i, j: (0, j)
    return pl.pallas_call(
        _merge_kernel,
        out_shape=jax.ShapeDtypeStruct((t, d), BF16),
        grid=(t // tm, d // tn),
        in_specs=[pl.BlockSpec((tm, d), row), pl.BlockSpec((tm, ka), row), pl.BlockSpec((tm, kc), row),
                  pl.BlockSpec((d, tn), col), pl.BlockSpec((d, tn), col),
                  pl.BlockSpec((ka, tn), col), pl.BlockSpec((kc, tn), col),
                  pl.BlockSpec((1, tn), col), pl.BlockSpec((1, tn), col)],
        out_specs=pl.BlockSpec((tm, tn), lambda i, j: (i, j)),
        compiler_params=_cparams(("parallel", "arbitrary")),
        name="merge",
    )(h, a, cv, wga, wgb, wa, wc, ba, bb)


def _outproj_kernel(seq_ref, m_ref, w_ref, x_ref, g_ref, o_ref):
    del seq_ref
    o_ref[...] = x_ref[...] + g_ref[0] * _bdot(m_ref[...], w_ref[...])


def _outproj(m, w, x, gate, seq_of_tile, tm, tn):
    t, d = x.shape
    k = m.shape[1]
    return pl.pallas_call(
        _outproj_kernel,
        out_shape=jax.ShapeDtypeStruct((t, d), F32),
        grid_spec=pltpu.PrefetchScalarGridSpec(
            num_scalar_prefetch=1, grid=(t // tm, d // tn),
            in_specs=[pl.BlockSpec((tm, k), lambda i, j, s: (i, 0)),
                      pl.BlockSpec((k, tn), lambda i, j, s: (0, j)),
                      pl.BlockSpec((tm, tn), lambda i, j, s: (i, j)),
                      pl.BlockSpec((1, 1, tn), lambda i, j, s: (s[i], 0, j))],
            out_specs=pl.BlockSpec((tm, tn), lambda i, j, s: (i, j))),
        compiler_params=_cparams(("parallel", "arbitrary")),
        name="outproj",
    )(seq_of_tile, m, w, x, gate)


def _lagmat_kernel(c_ref, b_ref, o_ref):
    for g in range(c_ref.shape[0]):
        o_ref[g] = jnp.dot(c_ref[g], b_ref[g], preferred_element_type=F32,
                           precision=lax.Precision.HIGHEST)


def _lagmat(cs, bs):
    g2, c, n2 = cs.shape
    w = bs.shape[2]
    gb = _tile(g2, 8)
    return pl.pallas_call(
        _lagmat_kernel,
        out_shape=jax.ShapeDtypeStruct((g2, c, w), F32),
        grid=(g2 // gb,),
        in_specs=[pl.BlockSpec((gb, c, n2), lambda i: (i, 0, 0)),
                  pl.BlockSpec((gb, n2, w), lambda i: (i, 0, 0))],
        out_specs=pl.BlockSpec((gb, c, w), lambda i: (i, 0, 0)),
        compiler_params=_cparams(("arbitrary",)),
        name="ssm_lagmat",
    )(cs, bs)


def _ssm_tables(a_re, a_im, log_dt, b_re, b_im, c_re, c_im, d_skip):
    tc, ch = SSM_CHUNK, SSM_GROUP_CH
    _, g, n = a_re.shape
    dt = jnp.exp(log_dt)[:, :, None]
    zr, zi = a_re * dt, a_im * dt
    ks = jnp.arange(tc + 1, dtype=F32)[:, None, None, None]
    mag = jnp.exp(ks * zr[None])
    pw_re, pw_im = mag * jnp.cos(ks * zi[None]), mag * jnp.sin(ks * zi[None])
    lb_re, lb_im = pw_re[1], pw_im[1]
    den = a_re * a_re + a_im * a_im
    q_re = ((lb_re - 1.0) * a_re + lb_im * a_im) / den
    q_im = (lb_im * a_re - (lb_re - 1.0) * a_im) / den
    bb_re = q_re[..., None] * b_re - q_im[..., None] * b_im
    bb_im = q_re[..., None] * b_im + q_im[..., None] * b_re
    p_re, p_im = pw_re[:tc, ..., None], pw_im[:tc, ..., None]
    bk_re = p_re * bb_re[None] - p_im * bb_im[None]
    bk_im = p_re * bb_im[None] + p_im * bb_re[None]

    bs = jnp.concatenate([bk_re, bk_im], axis=3)
    bs = bs.transpose(1, 2, 3, 0, 4).reshape(2 * g, 2 * n, tc * ch)
    cs = jnp.concatenate([c_re, -c_im], axis=2)
    cs = jnp.broadcast_to(cs[None], (2, g, ch, 2 * n)).reshape(2 * g, ch, 2 * n)
    m = _lagmat(cs, bs).reshape(2, g, ch, tc, ch)

    s_idx = np.arange(tc)[:, None]
    t_idx = np.arange(tc)[None, :]
    lag_f = np.clip(t_idx - s_idx, 0, tc - 1)
    lag_b = np.clip(s_idx - t_idx, 0, tc - 1)
    mf = m[0][:, :, lag_f, :] * jnp.asarray(t_idx >= s_idx, F32)[None, None, :, :, None]
    mb = m[1][:, :, lag_b, :] * jnp.asarray(s_idx >= t_idx, F32)[None, None, :, :, None]
    toep = (mf + mb).transpose(0, 2, 4, 3, 1).reshape(g, tc * ch, tc * ch)

    def rows_sc(x):
        return x.transpose(1, 0, 3, 2).reshape(g, tc * ch, n)
    pin = jnp.concatenate([rows_sc(bk_re[::-1, 0]), rows_sc(bk_re[:, 1]),
                           rows_sc(bk_im[::-1, 0]), rows_sc(bk_im[:, 1])], axis=2)

    def w_tab(pr, pi):
        wr = c_re[None] * pr[:, :, None, :] - c_im[None] * pi[:, :, None, :]
        wi = c_re[None] * pi[:, :, None, :] + c_im[None] * pr[:, :, None, :]
        to_rows = lambda x: x.transpose(1, 3, 0, 2).reshape(g, n, tc * ch)
        return to_rows(wr), to_rows(-wi)
    qf_re, qf_im = w_tab(pw_re[1:tc + 1, 0], pw_im[1:tc + 1, 0])
    qb_re, qb_im = w_tab(pw_re[tc:0:-1, 1], pw_im[tc:0:-1, 1])
    qout = jnp.concatenate([qf_re, qb_re, qf_im, qb_im], axis=1)

    a_step_re = jnp.concatenate([pw_re[tc, 0], pw_re[tc, 1]], axis=1)
    a_step_im = jnp.concatenate([pw_im[tc, 0], pw_im[tc, 1]], axis=1)
    d_tile = jnp.tile(d_skip.reshape(g, 1, ch), (1, tc, 1)).reshape(g, tc * ch)
    return (toep.astype(BF16), pin.astype(BF16), qout.astype(BF16),
            a_step_re, a_step_im, d_tile)


def _gelu_tanh(x):
    return 0.5 * x * (1.0 + jnp.tanh(math.sqrt(2.0 / math.pi) * (x + 0.044715 * (x * x * x))))


def _ssm_kernel(u_ref, t_ref, p_ref, q_ref, are_ref, aim_ref, d_ref, o_ref,
                sin_re, sin_im, hf_re, hf_im, hb_re, hb_im):
    gb, nc, w = u_ref.shape
    half = w // 2
    for g in range(gb):
        s_in = _bdot(u_ref[g], p_ref[g])
        sin_re[pl.ds(g * nc, nc), :] = s_in[:, :half]
        sin_im[pl.ds(g * nc, nc), :] = s_in[:, half:]

    a_re = are_ref[...]
    a_im = aim_ref[...]
    is_fwd = lax.broadcasted_iota(I32, (gb, half), 1) < (half // 2)

    def step(j, carry):
        s_re, s_im = carry
        at_f = pl.ds(j, gb, stride=nc)
        at_b = pl.ds(nc - 1 - j, gb, stride=nc)
        hf_re[at_f, :] = s_re
        hf_im[at_f, :] = s_im
        hb_re[at_b, :] = s_re
        hb_im[at_b, :] = s_im
        in_re = jnp.where(is_fwd, sin_re[at_f, :], sin_re[at_b, :])
        in_im = jnp.where(is_fwd, sin_im[at_f, :], sin_im[at_b, :])
        return (a_re * s_re - a_im * s_im + in_re, a_re * s_im + a_im * s_re + in_im)

    zero = jnp.zeros((gb, half), F32)
    lax.fori_loop(0, nc, step, (zero, zero))

    fwd_lane = lax.broadcasted_iota(I32, (nc, half), 1) < (half // 2)
    for g in range(gb):
        u = u_ref[g]
        rows = pl.ds(g * nc, nc)
        hin = jnp.concatenate([jnp.where(fwd_lane, hf_re[rows, :], hb_re[rows, :]),
                               jnp.where(fwd_lane, hf_im[rows, :], hb_im[rows, :])], axis=1)
        y = _bdot(u, t_ref[g]) + _bdot(hin.astype(BF16), q_ref[g]) + d_ref[pl.ds(g, 1), :] * u.astype(F32)
        o_ref[g] = _gelu_tanh(y).astype(o_ref.dtype)


def _ssm_groups(ug, tabs, n_seq, gb):
    toep, pin, qout, a_re, a_im, d_tile = tabs
    g, rows, w = ug.shape
    nc = rows // n_seq
    half = w // 2
    tab_spec = pl.BlockSpec((gb, w, w), lambda i, b: (i, 0, 0))
    vec_spec = lambda width: pl.BlockSpec((gb, width), lambda i, b: (i, 0))
    return pl.pallas_call(
        _ssm_kernel,
        out_shape=jax.ShapeDtypeStruct((g, rows, w), BF16),
        grid=(g // gb, n_seq),
        in_specs=[pl.BlockSpec((gb, nc, w), lambda i, b: (i, b, 0)),
                  tab_spec, tab_spec, tab_spec,
                  vec_spec(half), vec_spec(half), vec_spec(w)],
        out_specs=pl.BlockSpec((gb, nc, w), lambda i, b: (i, b, 0)),
        scratch_shapes=[pltpu.VMEM((gb * nc, half), F32)] * 6,
        compiler_params=_cparams(("parallel", "arbitrary")),
        name="ssm",
    )(ug, toep, pin, qout, a_re, a_im, d_tile)


def _ssm_branch(u, tabs, seqs):
    t, wch = u.shape
    g = wch // SSM_GROUP_CH
    tc = SSM_CHUNK
    vec = tc * SSM_GROUP_CH
    ug = u.reshape(t // tc, tc, g, SSM_GROUP_CH).transpose(2, 0, 1, 3).reshape(g, t // tc, vec)
    outs = []
    row0 = 0
    for n_seq, seq_len in seqs:
        rows = n_seq * seq_len // tc
        outs.append(_ssm_groups(ug[:, row0:row0 + rows], tabs, n_seq, _tile(g, 8)))
        row0 += rows
    yg = jnp.concatenate(outs, axis=1)
    return yg.reshape(g, t // tc, tc, SSM_GROUP_CH).transpose(1, 2, 0, 3).reshape(t, wch)


def _conv_kernel(first_ref, last_ref, prev_ref, cur_ref, next_ref, w_ref, b_ref, g_ref, beta_ref,
                 o_ref, xs_ref, acc_ref, *, halo, row_chunk):
    i = pl.program_id(0)
    r, c = cur_ref.shape
    kw = w_ref.shape[0]
    half = kw // 2
    n_shift = xs_ref.shape[0]
    keep_prev = jnp.where(first_ref[i] == 0, 1.0, 0.0)
    keep_next = jnp.where(last_ref[i] == 0, 1.0, 0.0)
    xs_ref[0, pl.ds(0, halo), :] = prev_ref[...].astype(F32) * keep_prev
    xs_ref[0, pl.ds(halo, r), :] = cur_ref[...].astype(F32)
    xs_ref[0, pl.ds(halo + r, halo), :] = next_ref[...].astype(F32) * keep_next
    span = r + 2 * halo - n_shift
    for b in range(1, n_shift):
        xs_ref[b, pl.ds(0, span), :] = xs_ref[0, pl.ds(b, span), :]

    def chunk(rc, _):
        r0 = pl.multiple_of(rc * row_chunk, row_chunk)
        acc = jnp.zeros((row_chunk, c), F32)
        for k in range(kw):
            off = halo - half + k
            b = off % n_shift
            acc = acc + w_ref[pl.ds(k, 1), :] * xs_ref[b, pl.ds(r0 + (off - b), row_chunk), :]
        acc_ref[pl.ds(r0, row_chunk), :] = acc
        return 0

    lax.fori_loop(0, r // row_chunk, chunk, 0)
    y = acc_ref[...] + b_ref[...]
    mu = jnp.mean(y, axis=-1, keepdims=True)
    yc = y - mu
    var = jnp.mean(yc * yc, axis=-1, keepdims=True)
    z = yc * lax.rsqrt(var + LN_EPS) * g_ref[...] + beta_ref[...]
    o_ref[...] = (z * _sigmoid(z)).astype(o_ref.dtype)


def _conv_ln(ub, conv_w, conv_b, ln_g, ln_b, first_of_tile, last_of_tile, r):
    t, c = ub.shape
    kw = conv_w.shape[0]
    halo = 16
    assert kw // 2 <= halo and r % halo == 0
    nb = r // halo
    n_halo_blocks = t // halo
    row_chunk = _tile(r, 8)
    vec = lambda i, f, l: (0, 0)
    return pl.pallas_call(
        functools.partial(_conv_kernel, halo=halo, row_chunk=row_chunk),
        out_shape=jax.ShapeDtypeStruct((t, c), BF16),
        grid_spec=pltpu.PrefetchScalarGridSpec(
            num_scalar_prefetch=2, grid=(t // r,),
            in_specs=[pl.BlockSpec((halo, c), lambda i, f, l: (jnp.maximum(i * nb - 1, 0), 0)),
                      pl.BlockSpec((r, c), lambda i, f, l: (i, 0)),
                      pl.BlockSpec((halo, c), lambda i, f, l: (jnp.minimum((i + 1) * nb, n_halo_blocks - 1), 0)),
                      pl.BlockSpec((kw, c), vec), pl.BlockSpec((1, c), vec),
                      pl.BlockSpec((1, c), vec), pl.BlockSpec((1, c), vec)],
            out_specs=pl.BlockSpec((r, c), lambda i, f, l: (i, 0)),
            scratch_shapes=[pltpu.VMEM((8, r + 2 * halo, c), F32), pltpu.VMEM((r, c), F32)]),
        compiler_params=_cparams(("arbitrary",)),
        name="conv_ln",
    )(first_of_tile, last_of_tile, ub, ub, ub, conv_w, conv_b.reshape(1, c),
      ln_g.reshape(1, c), ln_b.reshape(1, c))


def _seg_allreduce(x, lane, seg, op):
    n = x.shape[-1]
    s = 1
    while s < seg:
        up = pltpu.roll(x, n - s, 1)
        dn = pltpu.roll(x, s, 1)
        x = op(x, jnp.where((lane & s) == 0, up, dn))
        s *= 2
    return x


def _route_kernel(lg_ref, bias_ref, idx_ref, wt_ref, rank_ref, cnt_ref, carry_ref, *, n_groups, topk_groups, top_k):
    i = pl.program_id(0)
    tm, e = lg_ref.shape
    seg = e // n_groups
    lane = lax.broadcasted_iota(I32, (tm, e), 1)
    lane_f = lane.astype(F32)
    neg = jnp.float32(-jnp.inf)

    scores = _sigmoid(lg_ref[...])
    choice = scores + bias_ref[...]

    m1 = _seg_allreduce(choice, lane, seg, jnp.maximum)
    first = _seg_allreduce(jnp.where(choice == m1, lane_f, float(e)), lane, seg, jnp.minimum)
    m2 = _seg_allreduce(jnp.where(lane_f == first, neg, choice), lane, seg, jnp.maximum)
    gscore = m1 + m2

    beaten = jnp.zeros((tm, e), F32)
    for dshift in range(1, n_groups):
        other = pltpu.roll(gscore, dshift * seg, 1)
        other_is_lower = lane >= dshift * seg
        beaten = beaten + jnp.where(other_is_lower, jnp.where(other >= gscore, 1.0, 0.0),
                                    jnp.where(other > gscore, 1.0, 0.0))
    masked = jnp.where(beaten < topk_groups, choice, neg)

    idx_out = jnp.zeros((tm, e), F32)
    sc_out = jnp.zeros((tm, e), F32)
    sel = jnp.zeros((tm, e), F32)
    picks = []
    for k in range(top_k):
        m = jnp.max(masked, axis=1, keepdims=True)
        pick = jnp.min(jnp.where(masked == m, lane_f, float(e)), axis=1, keepdims=True)
        hit = lane_f == pick
        s_k = jnp.sum(jnp.where(hit, scores, 0.0), axis=1, keepdims=True)
        idx_out = jnp.where(lane == k, pick, idx_out)
        sc_out = jnp.where(lane == k, s_k, sc_out)
        sel = jnp.where(hit, 1.0, sel)
        masked = jnp.where(hit, neg, masked)
        picks.append(hit)
    denom = jnp.sum(sc_out, axis=1, keepdims=True)
    idx_ref[...] = idx_out.astype(I32)
    wt_ref[...] = sc_out / denom * ROUTED_SCALE

    @pl.when(i == 0)
    def _():
        carry_ref[...] = jnp.zeros_like(carry_ref)

    rows = lax.broadcasted_iota(I32, (tm, tm), 0)
    cols = lax.broadcasted_iota(I32, (tm, tm), 1)
    tri = jnp.where(cols < rows, 1.0, 0.0).astype(BF16)
    rank = _bdot(tri, sel.astype(BF16)) + carry_ref[...]
    rank_out = jnp.zeros((tm, e), F32)
    for k in range(top_k):
        r_k = jnp.sum(jnp.where(picks[k], rank, 0.0), axis=1, keepdims=True)
        rank_out = jnp.where(lane == k, r_k, rank_out)
    rank_ref[...] = rank_out.astype(I32)
    carry_ref[...] = carry_ref[...] + jnp.sum(sel, axis=0, keepdims=True)
    cnt_ref[...] = carry_ref[...].astype(I32)


def _route(logits, bias, tm):
    t, e = logits.shape
    row = lambda i: (i, 0)
    fixed = lambda i: (0, 0)
    return pl.pallas_call(
        functools.partial(_route_kernel, n_groups=N_ROUTE_GROUPS, topk_groups=TOPK_ROUTE_GROUPS, top_k=TOP_K),
        out_shape=(jax.ShapeDtypeStruct((t, e), I32), jax.ShapeDtypeStruct((t, e), F32),
                   jax.ShapeDtypeStruct((t, e), I32), jax.ShapeDtypeStruct((1, e), I32)),
        grid=(t // tm,),
        in_specs=[pl.BlockSpec((tm, e), row), pl.BlockSpec((1, e), fixed)],
        out_specs=(pl.BlockSpec((tm, e), row), pl.BlockSpec((tm, e), row),
                   pl.BlockSpec((tm, e), row), pl.BlockSpec((1, e), fixed)),
        scratch_shapes=[pltpu.VMEM((1, e), F32)],
        compiler_params=_cparams(("arbitrary",)),
        name="route",
    )(logits, bias.reshape(1, e))


def _expert_kernel(blk_e_ref, nblk_ref, tok_hbm, x_hbm, wg_ref, wu_ref, wd_ref, o_ref,
                   tok_smem, xbuf, sem_tok, sem_row):
    del blk_e_ref
    b = pl.program_id(0)
    r, kh = xbuf.shape

    @pl.when(b < nblk_ref[0])
    def _():
        tok_cp = pltpu.make_async_copy(tok_hbm.at[b], tok_smem, sem_tok)
        tok_cp.start()
        tok_cp.wait()

        def row_copy(i):
            return pltpu.make_async_copy(x_hbm.at[pl.ds(tok_smem[0, i], 1)], xbuf.at[pl.ds(i, 1)], sem_row)

        def issue(i, _):
            row_copy(i).start()
            return 0

        def drain(i, _):
            row_copy(i).wait()
            return 0

        lax.fori_loop(0, r, issue, 0)
        lax.fori_loop(0, r, drain, 0)

        lo, hi = _unpack_bf16_pair(xbuf[...])
        lo = lo.astype(BF16)
        hi = hi.astype(BF16)
        gate = _bdot(lo, wg_ref[0, pl.ds(0, kh), :]) + _bdot(hi, wg_ref[0, pl.ds(kh, kh), :])
        up = _bdot(lo, wu_ref[0, pl.ds(0, kh), :]) + _bdot(hi, wu_ref[0, pl.ds(kh, kh), :])
        act = (gate * _sigmoid(gate) * up).astype(BF16)
        out = _bdot(act, wd_ref[0])
        o_ref[...] = _pack_bf16_pair(out[:, :kh], out[:, kh:])

    @pl.when(b >= nblk_ref[0])
    def _():
        o_ref[...] = jnp.zeros_like(o_ref)


def _experts(blk_e, nblk, row_tok, xp, wg, wu, wd, r):
    p = row_tok.shape[0] * r
    t, kh = xp.shape
    n_e, d, de = wg.shape
    last = lambda b, be, nb: jnp.minimum(b, nb[0] - 1)
    return pl.pallas_call(
        _expert_kernel,
        out_shape=jax.ShapeDtypeStruct((p, kh), U32),
        grid_spec=pltpu.PrefetchScalarGridSpec(
            num_scalar_prefetch=2, grid=(p // r,),
            in_specs=[pl.BlockSpec(memory_space=pl.ANY),
                      pl.BlockSpec(memory_space=pl.ANY),
                      pl.BlockSpec((1, d, de), lambda b, be, nb: (be[last(b, be, nb)], 0, 0)),
                      pl.BlockSpec((1, d, de), lambda b, be, nb: (be[last(b, be, nb)], 0, 0)),
                      pl.BlockSpec((1, de, d), lambda b, be, nb: (be[last(b, be, nb)], 0, 0))],
            out_specs=pl.BlockSpec((r, kh), lambda b, be, nb: (b, 0)),
            scratch_shapes=[pltpu.SMEM((1, r), I32), pltpu.VMEM((r, kh), U32),
                            pltpu.SemaphoreType.DMA, pltpu.SemaphoreType.DMA]),
        compiler_params=_cparams(("arbitrary",)),
        name="experts",
    )(blk_e, nblk, row_tok, xp, wg, wu, wd)


def _combine_kernel(seq_ref, dest_hbm, ys_hbm, wt_ref, act_ref, wsd_ref, x_ref, g2_ref, fn_ref, o_ref,
                    dest_smem, ybuf, sem_idx, sem_row, *, top_k):
    del seq_ref
    i = pl.program_id(0)
    tm, d = x_ref.shape
    kh = d // 2
    n = tm * top_k
    idx_cp = pltpu.make_async_copy(dest_hbm.at[i], dest_smem, sem_idx)
    idx_cp.start()
    idx_cp.wait()

    def row_copy(a):
        return pltpu.make_async_copy(ys_hbm.at[pl.ds(dest_smem[0, a], 1)], ybuf.at[pl.ds(a, 1)], sem_row)

    def issue(a, _):
        row_copy(a).start()
        return 0

    def drain(a, _):
        row_copy(a).wait()
        return 0

    lax.fori_loop(0, n, issue, 0)
    shared = _bdot(act_ref[...], wsd_ref[...])
    lax.fori_loop(0, n, drain, 0)

    wt = wt_ref[...]
    r_lo = jnp.zeros((tm, kh), F32)
    r_hi = jnp.zeros((tm, kh), F32)
    for k in range(top_k):
        lo, hi = _unpack_bf16_pair(ybuf[pl.ds(k * tm, tm), :])
        w_k = wt[:, k:k + 1]
        r_lo = r_lo + w_k * lo
        r_hi = r_hi + w_k * hi
    routed = jnp.concatenate([r_lo, r_hi], axis=1)
    y = x_ref[...] + g2_ref[0] * (shared + routed)
    ms = jnp.mean(y * y, axis=-1, keepdims=True)
    o_ref[...] = y * lax.rsqrt(ms + RMS_EPS) * fn_ref[...]


def _combine(dest_km, ys, wt, act, wsd, x1, g2, final_norm, seq_of_tile, tm):
    t, d = x1.shape
    e = wt.shape[1]
    ds_ = act.shape[1]
    kh = d // 2
    row = lambda i, s: (i, 0)
    fixed = lambda i, s: (0, 0)
    return pl.pallas_call(
        functools.partial(_combine_kernel, top_k=TOP_K),
        out_shape=jax.ShapeDtypeStruct((t, d), F32),
        grid_spec=pltpu.PrefetchScalarGridSpec(
            num_scalar_prefetch=1, grid=(t // tm,),
            in_specs=[pl.BlockSpec(memory_space=pl.ANY),
                      pl.BlockSpec(memory_space=pl.ANY),
                      pl.BlockSpec((tm, e), row),
                      pl.BlockSpec((tm, ds_), row),
                      pl.BlockSpec((ds_, d), fixed),
                      pl.BlockSpec((tm, d), row),
                      pl.BlockSpec((1, 1, d), lambda i, s: (s[i], 0, 0)),
                      pl.BlockSpec((1, d), fixed)],
            out_specs=pl.BlockSpec((tm, d), row),
            scratch_shapes=[pltpu.SMEM((1, tm * TOP_K), I32), pltpu.VMEM((tm * TOP_K, kh), U32),
                            pltpu.SemaphoreType.DMA, pltpu.SemaphoreType.DMA]),
        compiler_params=_cparams(("arbitrary",)),
        name="combine",
    )(seq_of_tile, dest_km, ys, wt, act, wsd, x1, g2, final_norm.reshape(1, d))


def _seq_tables(seqs, tile):
    sid, first, last = [], [], []
    s = 0
    for n_seq, seq_len in seqs:
        per = seq_len // tile
        for _ in range(n_seq):
            sid += [s] * per
            first += [1] + [0] * (per - 1)
            last += [0] * (per - 1) + [1]
            s += 1
    as_i32 = lambda v: jnp.asarray(np.asarray(v, np.int32))
    return as_i32(sid), as_i32(first), as_i32(last)


def _layer(x, c_all, seqs, p):
    t, d = x.shape
    w_ssm = p["ssm_d"].shape[0]
    w_conv = p["conv_b"].shape[0]
    n_e = p["router_w"].shape[1]
    min_seq = min(sl for _, sl in seqs)

    tm = _tile(min_seq, 1024)
    tn_row = _tile(min_seq, 256)
    seq_mm, _, _ = _seq_tables(seqs, tm)
    seq_nr, _, _ = _seq_tables(seqs, tn_row)

    n_seq_total = c_all.shape[0]
    ada = _ada(c_all, p["w_ada"], p["b_ada"])
    sh1, sc1, g1, sh2, sc2, g2 = [a.reshape(n_seq_total, 1, d) for a in jnp.split(ada, 6, axis=1)]

    bf = lambda w: w.astype(BF16)
    w_in = p["w_in"]
    h = _norm1(x, p["norm_mix"], sc1, sh1, seq_nr, tn_row)

    u_ssm = _proj(h, bf(w_in[:, :w_ssm]), tm, _tile(w_ssm, 1024), name="in_ssm")
    tabs = _ssm_tables(p["ssm_a_re"], p["ssm_a_im"], p["ssm_log_dt"], p["ssm_b_re"], p["ssm_b_im"],
                       p["ssm_c_re"], p["ssm_c_im"], p["ssm_d"])
    ya = _ssm_branch(u_ssm, tabs, seqs)
    wgl = p["w_ssm_glu"]
    a_act = _glu_proj(ya, bf(wgl[:, :w_ssm]), bf(wgl[:, w_ssm:]), tm, _tile(w_ssm, 1024), False, "ssm_glu")

    ub = _glu_proj(h, bf(w_in[:, w_ssm:w_ssm + w_conv]), bf(w_in[:, w_ssm + w_conv:]),
                   tm, _tile(w_conv, 512), False, "in_conv_glu")
    r_conv = _tile(min_seq, 256)
    _, first_c, last_c = _seq_tables(seqs, r_conv)
    cv = _conv_ln(ub, p["conv_w"], p["conv_b"], p["conv_norm_g"], p["conv_norm_b"], first_c, last_c, r_conv)

    wmg = p["w_merge_gate"]
    bmg = p["b_merge_gate"]
    tm_merge = _tile(min_seq, 512)
    m = _merge(h, a_act, cv, bf(wmg[:, :d]), bf(wmg[:, d:]), bf(p["w_branch_ssm"]), bf(p["w_branch_conv"]),
               bmg[:d].reshape(1, d), bmg[d:].reshape(1, d), tm_merge, _tile(d, 512))
    x1 = _outproj(m, bf(p["w_out"]), x, g1, seq_mm, tm, _tile(d, 512))

    h2, h2p, logits = _norm2(x1, p["norm_ffn"], sc2, sh2, p["router_w"], seq_nr, tn_row)
    tm_route = _tile(t, 256)
    top_idx, top_w, rank, counts = _route(logits, p["router_bias"], tm_route)
    top_idx = top_idx[:, :TOP_K]
    rank = rank[:, :TOP_K]

    r_blk = 256 if t * TOP_K >= 256 * n_e else 8
    counts = counts[0]
    padded = (counts + r_blk - 1) // r_blk * r_blk
    pad_end = jnp.cumsum(padded)
    pad_start = pad_end - padded
    n_blocks = -(-(t * TOP_K) // r_blk) + n_e
    dest = pad_start[top_idx] + rank
    tok_ids = jnp.broadcast_to(jnp.arange(t, dtype=I32)[:, None], (t, TOP_K))
    row_tok = jnp.zeros((n_blocks * r_blk,), I32).at[dest.reshape(-1)].set(tok_ids.reshape(-1))
    row_tok = row_tok.reshape(n_blocks, 1, r_blk)
    blk_start = jnp.arange(n_blocks, dtype=I32) * r_blk
    blk_e = jnp.minimum(jnp.searchsorted(pad_end, blk_start, side="right"), n_e - 1).astype(I32)
    nblk = (pad_end[-1] // r_blk).astype(I32).reshape(1)

    ys = _experts(blk_e, nblk, row_tok, h2p, bf(p["exp_w_gate"]), bf(p["exp_w_up"]), bf(p["exp_w_down"]), r_blk)

    act = _glu_proj(h2, bf(p["shared_w_gate"]), bf(p["shared_w_up"]), tm, p["shared_w_gate"].shape[1],
                    True, "shared_act")
    tm_c = _tile(min_seq, 128)
    seq_c, _, _ = _seq_tables(seqs, tm_c)
    dest_km = dest.reshape(t // tm_c, tm_c, TOP_K).transpose(0, 2, 1).reshape(t // tm_c, 1, tm_c * TOP_K)
    return _combine(dest_km, ys, top_w, act, bf(p["shared_w_down"]), x1, g2, p["final_norm"], seq_c, tm_c)


def kernel(x_prompt, x_sample, c_prompt, c_sample, w_ada, b_ada, norm_mix, w_in, ssm_a_re, ssm_a_im, ssm_log_dt, ssm_b_re, ssm_b_im, ssm_c_re, ssm_c_im, ssm_d, w_ssm_glu, w_branch_ssm, conv_w, conv_b, conv_norm_g, conv_norm_b, w_branch_conv, w_merge_gate, b_merge_gate, w_out, norm_ffn, router_w, router_bias, exp_w_gate, exp_w_up, exp_w_down, shared_w_gate, shared_w_up, shared_w_down, final_norm):
    assert w_ada.shape[0] == 1, "single-layer trunk"
    bp, lp, d = x_prompt.shape
    bs, ls, _ = x_sample.shape
    seqs = [(bp, lp), (bs, ls)]
    x = jnp.concatenate([x_prompt.reshape(bp * lp, d), x_sample.reshape(bs * ls, d)], axis=0)
    n_seq = bp + bs
    c_all = jnp.concatenate([c_prompt, c_sample, jnp.zeros((-n_seq % 8, d), F32)], axis=0)
    params = dict(
        w_ada=w_ada[0], b_ada=b_ada[0], norm_mix=norm_mix[0], w_in=w_in[0],
        ssm_a_re=ssm_a_re[0], ssm_a_im=ssm_a_im[0], ssm_log_dt=ssm_log_dt[0],
        ssm_b_re=ssm_b_re[0], ssm_b_im=ssm_b_im[0], ssm_c_re=ssm_c_re[0], ssm_c_im=ssm_c_im[0],
        ssm_d=ssm_d[0], w_ssm_glu=w_ssm_glu[0], w_branch_ssm=w_branch_ssm[0],
        conv_w=conv_w[0], conv_b=conv_b[0], conv_norm_g=conv_norm_g[0], conv_norm_b=conv_norm_b[0],
        w_branch_conv=w_branch_conv[0], w_merge_gate=w_merge_gate[0], b_merge_gate=b_merge_gate[0],
        w_out=w_out[0], norm_ffn=norm_ffn[0], router_w=router_w[0], router_bias=router_bias[0],
        exp_w_gate=exp_w_gate[0], exp_w_up=exp_w_up[0], exp_w_down=exp_w_down[0],
        shared_w_gate=shared_w_gate[0], shared_w_up=shared_w_up[0], shared_w_down=shared_w_down[0],
        final_norm=final_norm)
    y = _layer(x, c_all, seqs, params)
    return (y[:bp * lp].reshape(bp, lp, d), y[bp * lp:].reshape(bs, ls, d))
```

```python
import functools
import math

import jax
import jax.numpy as jnp
import numpy as np
from jax import lax
from jax.experimental import pallas as pl
from jax.experimental.pallas import tpu as pltpu

F32 = jnp.float32
BF16 = jnp.bfloat16
I32 = jnp.int32
U32 = jnp.uint32

SSM_GROUP_CH = 16
SSM_CHUNK = 16
N_ROUTE_GROUPS = 8
TOPK_ROUTE_GROUPS = 4
TOP_K = 8
ROUTED_SCALE = 2.5
RMS_EPS = 1e-6
LN_EPS = 1e-5

V7X_VMEM_LIMIT = 56 * 1024 * 1024
LANES = 128


def _cparams(sem, vmem=V7X_VMEM_LIMIT):
    return pltpu.CompilerParams(dimension_semantics=sem, vmem_limit_bytes=vmem)


def _tile(n, want):
    t = min(n, want)
    while n % t:
        t -= 1
    return t


def _sigmoid(x):
    return 1.0 / (1.0 + jnp.exp(-x))


def _bdot(a, b):
    return jnp.dot(a, b, preferred_element_type=F32)


def _ada_kernel(c_ref, w_ref, b_ref, o_ref):
    c = c_ref[...]
    s = (c * _sigmoid(c)).astype(BF16)
    o_ref[...] = _bdot(s, w_ref[...].astype(BF16)) + b_ref[...]


def _ada(c_all, w_ada, b_ada):
    nseq, d = c_all.shape
    n = w_ada.shape[1]
    tn = _tile(n, 512)
    return pl.pallas_call(
        _ada_kernel,
        out_shape=jax.ShapeDtypeStruct((nseq, n), F32),
        grid=(n // tn,),
        in_specs=[pl.BlockSpec((nseq, d), lambda j: (0, 0)),
                  pl.BlockSpec((d, tn), lambda j: (0, j)),
                  pl.BlockSpec((1, tn), lambda j: (0, j))],
        out_specs=pl.BlockSpec((nseq, tn), lambda j: (0, j)),
        compiler_params=_cparams(("arbitrary",)),
        name="ada",
    )(c_all, w_ada, b_ada.reshape(1, n))


def _pack_bf16_pair(lo, hi):
    lo_b = pltpu.bitcast(lo.astype(BF16).astype(F32), U32) >> 16
    hi_b = pltpu.bitcast(hi.astype(BF16).astype(F32), U32) & jnp.uint32(0xFFFF0000)
    return hi_b | lo_b


def _unpack_bf16_pair(w):
    lo = pltpu.bitcast(w << 16, F32)
    hi = pltpu.bitcast(w & jnp.uint32(0xFFFF0000), F32)
    return lo, hi


def _modulated_norm(x, g, sc, sh):
    ms = jnp.mean(x * x, axis=-1, keepdims=True)
    return x * lax.rsqrt(ms + RMS_EPS) * g * (1.0 + sc) + sh


def _norm1_kernel(seq_ref, x_ref, g_ref, sc_ref, sh_ref, o_ref):
    del seq_ref
    o_ref[...] = _modulated_norm(x_ref[...], g_ref[...], sc_ref[0], sh_ref[0]).astype(BF16)


def _norm2_kernel(seq_ref, x_ref, g_ref, sc_ref, sh_ref, rw_ref, h_ref, hp_ref, lg_ref):
    del seq_ref
    h = _modulated_norm(x_ref[...], g_ref[...], sc_ref[0], sh_ref[0])
    half = h.shape[1] // 2
    h_ref[...] = h.astype(BF16)
    hp_ref[...] = _pack_bf16_pair(h[:, :half], h[:, half:])
    lg_ref[...] = jnp.dot(h, rw_ref[...], preferred_element_type=F32,
                          precision=lax.Precision.HIGHEST)


def _norm1(x, g, sc, sh, seq_of_tile, tm):
    t, d = x.shape
    row = lambda i, s: (i, 0)
    per_seq = lambda i, s: (s[i], 0, 0)
    return pl.pallas_call(
        _norm1_kernel,
        out_shape=jax.ShapeDtypeStruct((t, d), BF16),
        grid_spec=pltpu.PrefetchScalarGridSpec(
            num_scalar_prefetch=1, grid=(t // tm,),
            in_specs=[pl.BlockSpec((tm, d), row),
                      pl.BlockSpec((1, d), lambda i, s: (0, 0)),
                      pl.BlockSpec((1, 1, d), per_seq),
                      pl.BlockSpec((1, 1, d), per_seq)],
            out_specs=pl.BlockSpec((tm, d), row)),
        compiler_params=_cparams(("arbitrary",)),
        name="norm1",
    )(seq_of_tile, x, g.reshape(1, d), sc, sh)


def _norm2(x, g, sc, sh, router_w, seq_of_tile, tm):
    t, d = x.shape
    e = router_w.shape[1]
    row = lambda i, s: (i, 0)
    per_seq = lambda i, s: (s[i], 0, 0)
    return pl.pallas_call(
        _norm2_kernel,
        out_shape=(jax.ShapeDtypeStruct((t, d), BF16),
                   jax.ShapeDtypeStruct((t, d // 2), U32),
                   jax.ShapeDtypeStruct((t, e), F32)),
        grid_spec=pltpu.PrefetchScalarGridSpec(
            num_scalar_prefetch=1, grid=(t // tm,),
            in_specs=[pl.BlockSpec((tm, d), row),
                      pl.BlockSpec((1, d), lambda i, s: (0, 0)),
                      pl.BlockSpec((1, 1, d), per_seq),
                      pl.BlockSpec((1, 1, d), per_seq),
                      pl.BlockSpec((d, e), lambda i, s: (0, 0))],
            out_specs=(pl.BlockSpec((tm, d), row),
                       pl.BlockSpec((tm, d // 2), row),
                       pl.BlockSpec((tm, e), row))),
        compiler_params=_cparams(("arbitrary",)),
        name="norm2_router",
    )(seq_of_tile, x, g.reshape(1, d), sc, sh, router_w)


def _proj_kernel(x_ref, w_ref, o_ref):
    o_ref[...] = _bdot(x_ref[...], w_ref[...]).astype(o_ref.dtype)


def _proj(x, w, tm, tn, out_dtype=BF16, name="proj"):
    t, k = x.shape
    n = w.shape[1]
    return pl.pallas_call(
        _proj_kernel,
        out_shape=jax.ShapeDtypeStruct((t, n), out_dtype),
        grid=(t // tm, n // tn),
        in_specs=[pl.BlockSpec((tm, k), lambda i, j: (i, 0)),
                  pl.BlockSpec((k, tn), lambda i, j: (0, j))],
        out_specs=pl.BlockSpec((tm, tn), lambda i, j: (i, j)),
        compiler_params=_cparams(("parallel", "arbitrary")),
        name=name,
    )(x, w)


def _glu_kernel(x_ref, wa_ref, wb_ref, o_ref, *, silu_first):
    x = x_ref[...].astype(BF16)
    a = _bdot(x, wa_ref[...])
    b = _bdot(x, wb_ref[...])
    if silu_first:
        y = a * _sigmoid(a) * b
    else:
        y = a * _sigmoid(b)
    o_ref[...] = y.astype(o_ref.dtype)


def _glu_proj(x, wa, wb, tm, tn, silu_first, name):
    t, k = x.shape
    n = wa.shape[1]
    return pl.pallas_call(
        functools.partial(_glu_kernel, silu_first=silu_first),
        out_shape=jax.ShapeDtypeStruct((t, n), BF16),
        grid=(t // tm, n // tn),
        in_specs=[pl.BlockSpec((tm, k), lambda i, j: (i, 0)),
                  pl.BlockSpec((k, tn), lambda i, j: (0, j)),
                  pl.BlockSpec((k, tn), lambda i, j: (0, j))],
        out_specs=pl.BlockSpec((tm, tn), lambda i, j: (i, j)),
        compiler_params=_cparams(("parallel", "arbitrary")),
        name=name,
    )(x, wa, wb)


def _merge_kernel(h_ref, a_ref, c_ref, wga_ref, wgb_ref, wa_ref, wc_ref, ba_ref, bb_ref, o_ref):
    h = h_ref[...]
    ga = _sigmoid(_bdot(h, wga_ref[...]) + ba_ref[...])
    gb = _sigmoid(_bdot(h, wgb_ref[...]) + bb_ref[...])
    ya = _bdot(a_ref[...], wa_ref[...])
    yb = _bdot(c_ref[...], wc_ref[...])
    o_ref[...] = (ga * ya + gb * yb).astype(o_ref.dtype)


def _merge(h, a, cv, wga, wgb, wa, wc, ba, bb, tm, tn):
    t, d = h.shape
    ka = a.shape[1]
    kc = cv.shape[1]
    row = lambda i, j: (i, 0)
    col = lambda i, j: (0, j)
    return pl.pallas_call(
        _merge_kernel,
        out_shape=jax.ShapeDtypeStruct((t, d), BF16),
        grid=(t // tm, d // tn),
        in_specs=[pl.BlockSpec((tm, d), row), pl.BlockSpec((tm, ka), row), pl.BlockSpec((tm, kc), row),
                  pl.BlockSpec((d, tn), col), pl.BlockSpec((d, tn), col),
                  pl.BlockSpec((ka, tn), col), pl.BlockSpec((kc, tn), col),
                  pl.BlockSpec((1, tn), col), pl.BlockSpec((1, tn), col)],
        out_specs=pl.BlockSpec((tm, tn), lambda i, j: (i, j)),
        compiler_params=_cparams(("parallel", "arbitrary")),
        name="merge",
    )(h, a, cv, wga, wgb, wa, wc, ba, bb)


def _outproj_kernel(seq_ref, m_ref, w_ref, x_ref, g_ref, o_ref):
    del seq_ref
    o_ref[...] = x_ref[...] + g_ref[0] * _bdot(m_ref[...], w_ref[...])


def _outproj(m, w, x, gate, seq_of_tile, tm, tn):
    t, d = x.shape
    k = m.shape[1]
    return pl.pallas_call(
        _outproj_kernel,
        out_shape=jax.ShapeDtypeStruct((t, d), F32),
        grid_spec=pltpu.PrefetchScalarGridSpec(
            num_scalar_prefetch=1, grid=(t // tm, d // tn),
            in_specs=[pl.BlockSpec((tm, k), lambda i, j, s: (i, 0)),
                      pl.BlockSpec((k, tn), lambda i, j, s: (0, j)),
                      pl.BlockSpec((tm, tn), lambda i, j, s: (i, j)),
                      pl.BlockSpec((1, 1, tn), lambda i, j, s: (s[i], 0, j))],
            out_specs=pl.BlockSpec((tm, tn), lambda i, j, s: (i, j))),
        compiler_params=_cparams(("parallel", "arbitrary")),
        name="outproj",
    )(seq_of_tile, m, w, x, gate)


def _lagmat_kernel(c_ref, b_ref, o_ref):
    for g in range(c_ref.shape[0]):
        o_ref[g] = jnp.dot(c_ref[g], b_ref[g], preferred_element_type=F32,
                           precision=lax.Precision.HIGHEST)


def _lagmat(cs, bs):
    g2, c, n2 = cs.shape
    w = bs.shape[2]
    gb = _tile(g2, 8)
    return pl.pallas_call(
        _lagmat_kernel,
        out_shape=jax.ShapeDtypeStruct((g2, c, w), F32),
        grid=(g2 // gb,),
        in_specs=[pl.BlockSpec((gb, c, n2), lambda i: (i, 0, 0)),
                  pl.BlockSpec((gb, n2, w), lambda i: (i, 0, 0))],
        out_specs=pl.BlockSpec((gb, c, w), lambda i: (i, 0, 0)),
        compiler_params=_cparams(("arbitrary",)),
        name="ssm_lagmat",
    )(cs, bs)


def _ssm_tables(a_re, a_im, log_dt, b_re, b_im, c_re, c_im, d_skip):
    tc, ch = SSM_CHUNK, SSM_GROUP_CH
    _, g, n = a_re.shape
    dt = jnp.exp(log_dt)[:, :, None]
    zr, zi = a_re * dt, a_im * dt
    ks = jnp.arange(tc + 1, dtype=F32)[:, None, None, None]
    mag = jnp.exp(ks * zr[None])
    pw_re, pw_im = mag * jnp.cos(ks * zi[None]), mag * jnp.sin(ks * zi[None])
    lb_re, lb_im = pw_re[1], pw_im[1]
    den = a_re * a_re + a_im * a_im
    q_re = ((lb_re - 1.0) * a_re + lb_im * a_im) / den
    q_im = (lb_im * a_re - (lb_re - 1.0) * a_im) / den
    bb_re = q_re[..., None] * b_re - q_im[..., None] * b_im
    bb_im = q_re[..., None] * b_im + q_im[..., None] * b_re
    p_re, p_im = pw_re[:tc, ..., None], pw_im[:tc, ..., None]
    bk_re = p_re * bb_re[None] - p_im * bb_im[None]
    bk_im = p_re * bb_im[None] + p_im * bb_re[None]

    bs = jnp.concatenate([bk_re, bk_im], axis=3)
    bs = bs.transpose(1, 2, 3, 0, 4).reshape(2 * g, 2 * n, tc * ch)
    cs = jnp.concatenate([c_re, -c_im], axis=2)
    cs = jnp.broadcast_to(cs[None], (2, g, ch, 2 * n)).reshape(2 * g, ch, 2 * n)
    m = _lagmat(cs, bs).reshape(2, g, ch, tc, ch)

    s_idx = np.arange(tc)[:, None]
    t_idx = np.arange(tc)[None, :]
    lag_f = np.clip(t_idx - s_idx, 0, tc - 1)
    lag_b = np.clip(s_idx - t_idx, 0, tc - 1)
    mf = m[0][:, :, lag_f, :] * jnp.asarray(t_idx >= s_idx, F32)[None, None, :, :, None]
    mb = m[1][:, :, lag_b, :] * jnp.asarray(s_idx >= t_idx, F32)[None, None, :, :, None]
    toep = (mf + mb).transpose(0, 2, 4, 3, 1).reshape(g, tc * ch, tc * ch)

    def rows_sc(x):
        return x.transpose(1, 0, 3, 2).reshape(g, tc * ch, n)
    pin = jnp.concatenate([rows_sc(bk_re[::-1, 0]), rows_sc(bk_re[:, 1]),
                           rows_sc(bk_im[::-1, 0]), rows_sc(bk_im[:, 1])], axis=2)

    def w_tab(pr, pi):
        wr = c_re[None] * pr[:, :, None, :] - c_im[None] * pi[:, :, None, :]
        wi = c_re[None] * pi[:, :, None, :] + c_im[None] * pr[:, :, None, :]
        to_rows = lambda x: x.transpose(1, 3, 0, 2).reshape(g, n, tc * ch)
        return to_rows(wr), to_rows(-wi)
    qf_re, qf_im = w_tab(pw_re[1:tc + 1, 0], pw_im[1:tc + 1, 0])
    qb_re, qb_im = w_tab(pw_re[tc:0:-1, 1], pw_im[tc:0:-1, 1])
    qout = jnp.concatenate([qf_re, qb_re, qf_im, qb_im], axis=1)

    a_step_re = jnp.concatenate([pw_re[tc, 0], pw_re[tc, 1]], axis=1)
    a_step_im = jnp.concatenate([pw_im[tc, 0], pw_im[tc, 1]], axis=1)
    d_tile = jnp.tile(d_skip.reshape(g, 1, ch), (1, tc, 1)).reshape(g, tc * ch)
    return (toep.astype(BF16), pin.astype(BF16), qout.astype(BF16),
            a_step_re, a_step_im, d_tile)


def _gelu_tanh(x):
    return 0.5 * x * (1.0 + jnp.tanh(math.sqrt(2.0 / math.pi) * (x + 0.044715 * (x * x * x))))


def _ssm_kernel(u_ref, t_ref, p_ref, q_ref, are_ref, aim_ref, d_ref, o_ref,
                ug_ref, yg_ref, sin_re, sin_im, hf_re, hf_im, hb_re, hb_im):
    gb, w, _ = t_ref.shape
    tc, ch = SSM_CHUNK, SSM_GROUP_CH
    nc = u_ref.shape[0] // tc
    half = w // 2
    n_col = w // LANES
    slots = LANES // ch
    slot = lax.broadcasted_iota(I32, (8, LANES), 1) // ch
    in_slot = [slot == k for k in range(slots)]

    def to_chunks(rc, _):
        cols = [[jnp.zeros((8, LANES), F32) for _ in range(n_col)] for _ in range(gb)]
        for t in range(tc):
            x = u_ref[pl.ds(rc * (8 * tc) + t, 8, stride=tc), :]
            rolled = [x] + [pltpu.roll(x, k * ch, 1) for k in range(1, slots)]
            ts, col = t % slots, t // slots
            for g in range(gb):
                cols[g][col] = jnp.where(in_slot[ts], rolled[(ts - g) % slots], cols[g][col])
        for g in range(gb):
            ug_ref[g, pl.ds(rc * 8, 8), :] = jnp.concatenate(cols[g], axis=1)
        return 0

    lax.fori_loop(0, nc // 8, to_chunks, 0)

    for g in range(gb):
        s_in = _bdot(ug_ref[g].astype(BF16), p_ref[g])
        sin_re[pl.ds(g * nc, nc), :] = s_in[:, :half]
        sin_im[pl.ds(g * nc, nc), :] = s_in[:, half:]

    a_re = are_ref[...]
    a_im = aim_ref[...]
    is_fwd = lax.broadcasted_iota(I32, (gb, half), 1) < (half // 2)

    def step(j, carry):
        s_re, s_im = carry
        at_f = pl.ds(j, gb, stride=nc)
        at_b = pl.ds(nc - 1 - j, gb, stride=nc)
        hf_re[at_f, :] = s_re
        hf_im[at_f, :] = s_im
        hb_re[at_b, :] = s_re
        hb_im[at_b, :] = s_im
        in_re = jnp.where(is_fwd, sin_re[at_f, :], sin_re[at_b, :])
        in_im = jnp.where(is_fwd, sin_im[at_f, :], sin_im[at_b, :])
        return (a_re * s_re - a_im * s_im + in_re, a_re * s_im + a_im * s_re + in_im)

    zero = jnp.zeros((gb, half), F32)
    lax.fori_loop(0, nc, step, (zero, zero))

    fwd_lane = lax.broadcasted_iota(I32, (nc, half), 1) < (half // 2)
    for g in range(gb):
        u = ug_ref[g]
        rows = pl.ds(g * nc, nc)
        hin = jnp.concatenate([jnp.where(fwd_lane, hf_re[rows, :], hb_re[rows, :]),
                               jnp.where(fwd_lane, hf_im[rows, :], hb_im[rows, :])], axis=1)
        y = _bdot(u.astype(BF16), t_ref[g]) + _bdot(hin.astype(BF16), q_ref[g]) + d_ref[pl.ds(g, 1), :] * u
        yg_ref[g] = _gelu_tanh(y)

    def from_chunks(rc, _):
        for col in range(n_col):
            z = [jnp.zeros((8, LANES), F32) for _ in range(slots)]
            for g in range(gb):
                y = yg_ref[g, pl.ds(rc * 8, 8), pl.ds(col * LANES, LANES)]
                rolled = [y] + [pltpu.roll(y, k * ch, 1) for k in range(1, slots)]
                for ts in range(slots):
                    z[ts] = jnp.where(in_slot[g], rolled[(g - ts) % slots], z[ts])
            for ts in range(slots):
                o_ref[pl.ds(rc * (8 * tc) + col * slots + ts, 8, stride=tc), :] = z[ts]
        return 0

    lax.fori_loop(0, nc // 8, from_chunks, 0)


def _ssm_seqs(u, tabs, row0, n_seq, seq_len):
    toep, pin, qout, a_re, a_im, d_tile = tabs
    _, wch = u.shape
    w = toep.shape[1]
    half = w // 2
    gb = LANES // SSM_GROUP_CH
    nc = seq_len // SSM_CHUNK
    assert nc % 8 == 0 and row0 % seq_len == 0 and wch % LANES == 0
    blk0 = row0 // seq_len
    tab_spec = pl.BlockSpec((gb, w, w), lambda i, b: (i, 0, 0))
    vec_spec = lambda width: pl.BlockSpec((gb, width), lambda i, b: (i, 0))
    return pl.pallas_call(
        _ssm_kernel,
        out_shape=jax.ShapeDtypeStruct((n_seq * seq_len, wch), F32),
        grid=(wch // LANES, n_seq),
        in_specs=[pl.BlockSpec((seq_len, LANES), lambda i, b: (blk0 + b, i)),
                  tab_spec, tab_spec, tab_spec,
                  vec_spec(half), vec_spec(half), vec_spec(w)],
        out_specs=pl.BlockSpec((seq_len, LANES), lambda i, b: (b, i)),
        scratch_shapes=[pltpu.VMEM((gb, nc, w), F32)] * 2 + [pltpu.VMEM((gb * nc, half), F32)] * 6,
        compiler_params=_cparams(("parallel", "arbitrary")),
        name="ssm",
    )(u, toep, pin, qout, a_re, a_im, d_tile)


def _ssm_branch(u, tabs, seqs):
    outs = []
    row0 = 0
    for n_seq, seq_len in seqs:
        outs.append(_ssm_seqs(u, tabs, row0, n_seq, seq_len))
        row0 += n_seq * seq_len
    return jnp.concatenate(outs, axis=0)


def _conv_kernel(first_ref, last_ref, prev_ref, cur_ref, next_ref, w_ref, b_ref, g_ref, beta_ref,
                 o_ref, xs_ref, acc_ref, *, halo, row_chunk):
    i = pl.program_id(0)
    r, c = cur_ref.shape
    kw = w_ref.shape[0]
    half = kw // 2
    n_shift = xs_ref.shape[0]
    keep_prev = jnp.where(first_ref[i] == 0, 1.0, 0.0)
    keep_next = jnp.where(last_ref[i] == 0, 1.0, 0.0)
    xs_ref[0, pl.ds(0, halo), :] = prev_ref[...].astype(F32) * keep_prev
    xs_ref[0, pl.ds(halo, r), :] = cur_ref[...].astype(F32)
    xs_ref[0, pl.ds(halo + r, halo), :] = next_ref[...].astype(F32) * keep_next
    span = r + 2 * halo - n_shift
    for b in range(1, n_shift):
        xs_ref[b, pl.ds(0, span), :] = xs_ref[0, pl.ds(b, span), :]

    col_chunk = _tile(c, 4 * LANES)
    n_cc = c // col_chunk

    def chunk(it, _):
        r0 = pl.multiple_of((it // n_cc) * row_chunk, row_chunk)
        c0 = pl.multiple_of((it % n_cc) * col_chunk, col_chunk)
        cols = pl.ds(c0, col_chunk)
        acc = jnp.zeros((row_chunk, col_chunk), F32)
        for k in range(kw):
            off = halo - half + k
            b = off % n_shift
            acc = acc + w_ref[pl.ds(k, 1), cols] * xs_ref[b, pl.ds(r0 + (off - b), row_chunk), cols]
        acc_ref[pl.ds(r0, row_chunk), cols] = acc
        return 0

    lax.fori_loop(0, (r // row_chunk) * n_cc, chunk, 0)
    ln_rows = _tile(r, 16)

    def norm_rows(it, _):
        rows = pl.ds(pl.multiple_of(it * ln_rows, ln_rows), ln_rows)
        y = acc_ref[rows, :] + b_ref[...]
        mu = jnp.mean(y, axis=-1, keepdims=True)
        yc = y - mu
        var = jnp.mean(yc * yc, axis=-1, keepdims=True)
        z = yc * lax.rsqrt(var + LN_EPS) * g_ref[...] + beta_ref[...]
        o_ref[rows, :] = (z * _sigmoid(z)).astype(o_ref.dtype)
        return 0

    lax.fori_loop(0, r // ln_rows, norm_rows, 0, unroll=2)


def _conv_ln(ub, conv_w, conv_b, ln_g, ln_b, first_of_tile, last_of_tile, r):
    t, c = ub.shape
    kw = conv_w.shape[0]
    halo = 16
    assert kw // 2 <= halo and r % halo == 0
    nb = r // halo
    n_halo_blocks = t // halo
    row_chunk = _tile(r, 32)
    vec = lambda i, f, l: (0, 0)
    return pl.pallas_call(
        functools.partial(_conv_kernel, halo=halo, row_chunk=row_chunk),
        out_shape=jax.ShapeDtypeStruct((t, c), BF16),
        grid_spec=pltpu.PrefetchScalarGridSpec(
            num_scalar_prefetch=2, grid=(t // r,),
            in_specs=[pl.BlockSpec((halo, c), lambda i, f, l: (jnp.maximum(i * nb - 1, 0), 0)),
                      pl.BlockSpec((r, c), lambda i, f, l: (i, 0)),
                      pl.BlockSpec((halo, c), lambda i, f, l: (jnp.minimum((i + 1) * nb, n_halo_blocks - 1), 0)),
                      pl.BlockSpec((kw, c), vec), pl.BlockSpec((1, c), vec),
                      pl.BlockSpec((1, c), vec), pl.BlockSpec((1, c), vec)],
            out_specs=pl.BlockSpec((r, c), lambda i, f, l: (i, 0)),
            scratch_shapes=[pltpu.VMEM((8, r + 2 * halo, c), F32), pltpu.VMEM((r, c), F32)]),
        compiler_params=_cparams(("arbitrary",)),
        name="conv_ln",
    )(first_of_tile, last_of_tile, ub, ub, ub, conv_w, conv_b.reshape(1, c),
      ln_g.reshape(1, c), ln_b.reshape(1, c))


def _seg_allreduce(x, lane, seg, op):
    n = x.shape[-1]
    s = 1
    while s < seg:
        up = pltpu.roll(x, n - s, 1)
        dn = pltpu.roll(x, s, 1)
        x = op(x, jnp.where((lane & s) == 0, up, dn))
        s *= 2
    return x


def _route_kernel(lg_ref, bias_ref, idx_ref, wt_ref, rank_ref, cnt_ref, carry_ref, *, n_groups, topk_groups, top_k):
    i = pl.program_id(0)
    tm, e = lg_ref.shape
    seg = e // n_groups
    lane = lax.broadcasted_iota(I32, (tm, e), 1)
    lane_f = lane.astype(F32)
    neg = jnp.float32(-jnp.inf)

    scores = _sigmoid(lg_ref[...])
    choice = scores + bias_ref[...]

    m1 = _seg_allreduce(choice, lane, seg, jnp.maximum)
    first = _seg_allreduce(jnp.where(choice == m1, lane_f, float(e)), lane, seg, jnp.minimum)
    m2 = _seg_allreduce(jnp.where(lane_f == first, neg, choice), lane, seg, jnp.maximum)
    gscore = m1 + m2

    beaten = jnp.zeros((tm, e), F32)
    for dshift in range(1, n_groups):
        other = pltpu.roll(gscore, dshift * seg, 1)
        other_is_lower = lane >= dshift * seg
        beaten = beaten + jnp.where(other_is_lower, jnp.where(other >= gscore, 1.0, 0.0),
                                    jnp.where(other > gscore, 1.0, 0.0))
    masked = jnp.where(beaten < topk_groups, choice, neg)

    idx_out = jnp.zeros((tm, e), F32)
    sc_out = jnp.zeros((tm, e), F32)
    sel = jnp.zeros((tm, e), F32)
    picks = []
    for k in range(top_k):
        m = jnp.max(masked, axis=1, keepdims=True)
        pick = jnp.min(jnp.where(masked == m, lane_f, float(e)), axis=1, keepdims=True)
        hit = lane_f == pick
        s_k = jnp.sum(jnp.where(hit, scores, 0.0), axis=1, keepdims=True)
        idx_out = jnp.where(lane == k, pick, idx_out)
        sc_out = jnp.where(lane == k, s_k, sc_out)
        sel = jnp.where(hit, 1.0, sel)
        masked = jnp.where(hit, neg, masked)
        picks.append(hit)
    denom = jnp.sum(sc_out, axis=1, keepdims=True)
    idx_ref[...] = idx_out.astype(I32)
    wt_ref[...] = sc_out / denom * ROUTED_SCALE

    @pl.when(i == 0)
    def _():
        carry_ref[...] = jnp.zeros_like(carry_ref)

    rows = lax.broadcasted_iota(I32, (tm, tm), 0)
    cols = lax.broadcasted_iota(I32, (tm, tm), 1)
    tri = jnp.where(cols < rows, 1.0, 0.0).astype(BF16)
    rank = _bdot(tri, sel.astype(BF16)) + carry_ref[...]
    rank_out = jnp.zeros((tm, e), F32)
    for k in range(top_k):
        r_k = jnp.sum(jnp.where(picks[k], rank, 0.0), axis=1, keepdims=True)
        rank_out = jnp.where(lane == k, r_k, rank_out)
    rank_ref[...] = rank_out.astype(I32)
    carry_ref[...] = carry_ref[...] + jnp.sum(sel, axis=0, keepdims=True)
    cnt_ref[...] = carry_ref[...].astype(I32)


def _route(logits, bias, tm):
    t, e = logits.shape
    row = lambda i: (i, 0)
    fixed = lambda i: (0, 0)
    return pl.pallas_call(
        functools.partial(_route_kernel, n_groups=N_ROUTE_GROUPS, topk_groups=TOPK_ROUTE_GROUPS, top_k=TOP_K),
        out_shape=(jax.ShapeDtypeStruct((t, e), I32), jax.ShapeDtypeStruct((t, e), F32),
                   jax.ShapeDtypeStruct((t, e), I32), jax.ShapeDtypeStruct((1, e), I32)),
        grid=(t // tm,),
        in_specs=[pl.BlockSpec((tm, e), row), pl.BlockSpec((1, e), fixed)],
        out_specs=(pl.BlockSpec((tm, e), row), pl.BlockSpec((tm, e), row),
                   pl.BlockSpec((tm, e), row), pl.BlockSpec((1, e), fixed)),
        scratch_shapes=[pltpu.VMEM((1, e), F32)],
        compiler_params=_cparams(("arbitrary",)),
        name="route",
    )(logits, bias.reshape(1, e))


def _expert_kernel(blk_e_ref, nblk_ref, idx_hbm, x_hbm, wg_ref, wu_ref, wd_ref, ys_hbm,
                   idx_smem, xb0, xb1, ob0, ob1, sem_idx, sem_g, sem_s):
    del blk_e_ref
    b = pl.program_id(0)
    n = nblk_ref[0]
    r, kh = xb0.shape
    xbufs = (xb0, xb1)
    obufs = (ob0, ob1)
    dump0 = ys_hbm.shape[0] - 2 * r

    def idx_copy(blk, s):
        return pltpu.make_async_copy(idx_hbm.at[blk], idx_smem.at[s], sem_idx.at[s])

    def gather_start(s):
        for i in range(r):
            pltpu.make_async_copy(x_hbm.at[pl.ds(idx_smem[s, 0, i], 1)], xbufs[s].at[pl.ds(i, 1)],
                                  sem_g.at[s]).start()

    def gather_wait(s):
        for i in range(r):
            pltpu.make_async_copy(x_hbm.at[pl.ds(0, 1)], xbufs[s].at[pl.ds(i, 1)], sem_g.at[s]).wait()

    def scatter_start(s):
        for i in range(r):
            pltpu.make_async_copy(obufs[s].at[pl.ds(i, 1)], ys_hbm.at[pl.ds(idx_smem[s, 1, i], 1)],
                                  sem_s.at[s]).start()

    def scatter_wait(s):
        for i in range(r):
            pltpu.make_async_copy(obufs[s].at[pl.ds(i, 1)], ys_hbm.at[pl.ds(0, 1)], sem_s.at[s]).wait()

    @pl.when(b == 0)
    def _():
        ob1[...] = jnp.zeros_like(ob1)
        for h in range(2):
            fill = pltpu.make_async_copy(ob1, ys_hbm.at[pl.ds(dump0 + h * r, r)], sem_s.at[1])
            fill.start()
            fill.wait()
        first = idx_copy(0, 0)
        first.start()
        first.wait()
        gather_start(0)

        @pl.when(n > 1)
        def _():
            idx_copy(1, 1).start()

    def step(cur):
        nxt = 1 - cur
        gather_wait(cur)

        @pl.when(b + 1 < n)
        def _():
            idx_copy(b + 1, nxt).wait()
            gather_start(nxt)

        @pl.when(b >= 2)
        def _():
            scatter_wait(cur)

        lo, hi = _unpack_bf16_pair(xbufs[cur][...])
        lo = lo.astype(BF16)
        hi = hi.astype(BF16)
        gate = _bdot(lo, wg_ref[0, pl.ds(0, kh), :]) + _bdot(hi, wg_ref[0, pl.ds(kh, kh), :])
        up = _bdot(lo, wu_ref[0, pl.ds(0, kh), :]) + _bdot(hi, wu_ref[0, pl.ds(kh, kh), :])
        act = (gate * _sigmoid(gate) * up).astype(BF16)
        out = _bdot(act, wd_ref[0])
        obufs[cur][...] = _pack_bf16_pair(out[:, :kh], out[:, kh:])
        scatter_start(cur)

        @pl.when(b + 2 < n)
        def _():
            idx_copy(b + 2, cur).start()

        @pl.when(b == n - 1)
        def _():
            scatter_wait(cur)

            @pl.when(b >= 1)
            def _():
                scatter_wait(nxt)

    for parity in range(2):
        @pl.when((b < n) & ((b & 1) == parity))
        def _(parity=parity):
            step(parity)


def _experts(blk_e, nblk, row_idx, xp, wg, wu, wd, n_out_rows):
    n_blocks, _, r = row_idx.shape
    t, kh = xp.shape
    n_e, d, de = wg.shape
    last = lambda b, be, nb: jnp.minimum(b, nb[0] - 1)
    return pl.pallas_call(
        _expert_kernel,
        out_shape=jax.ShapeDtypeStruct((n_out_rows + 2 * r, kh), U32),
        grid_spec=pltpu.PrefetchScalarGridSpec(
            num_scalar_prefetch=2, grid=(n_blocks,),
            in_specs=[pl.BlockSpec(memory_space=pl.ANY),
                      pl.BlockSpec(memory_space=pl.ANY),
                      pl.BlockSpec((1, d, de), lambda b, be, nb: (be[last(b, be, nb)], 0, 0)),
                      pl.BlockSpec((1, d, de), lambda b, be, nb: (be[last(b, be, nb)], 0, 0)),
                      pl.BlockSpec((1, de, d), lambda b, be, nb: (be[last(b, be, nb)], 0, 0))],
            out_specs=pl.BlockSpec(memory_space=pl.ANY),
            scratch_shapes=[pltpu.SMEM((2, 2, r), I32)] + [pltpu.VMEM((r, kh), U32)] * 4
                           + [pltpu.SemaphoreType.DMA((2,))] * 3),
        compiler_params=_cparams(("arbitrary",)),
        name="experts",
    )(blk_e, nblk, row_idx, xp, wg, wu, wd)


def _combine_kernel(seq_ref, *refs, top_k):
    del seq_ref
    y_refs = refs[:top_k]
    wt_ref, act_ref, wsd_ref, x_ref, g2_ref, fn_ref, o_ref = refs[top_k:]
    tm, d = x_ref.shape
    kh = d // 2
    shared = _bdot(act_ref[...], wsd_ref[...])
    wt = wt_ref[...]
    r_lo = jnp.zeros((tm, kh), F32)
    r_hi = jnp.zeros((tm, kh), F32)
    for k in range(top_k):
        lo, hi = _unpack_bf16_pair(y_refs[k][...])
        w_k = wt[:, k:k + 1]
        r_lo = r_lo + w_k * lo
        r_hi = r_hi + w_k * hi
    routed = jnp.concatenate([r_lo, r_hi], axis=1)
    y = x_ref[...] + g2_ref[0] * (shared + routed)
    ms = jnp.mean(y * y, axis=-1, keepdims=True)
    o_ref[...] = y * lax.rsqrt(ms + RMS_EPS) * fn_ref[...]


def _combine(ys, wt, act, wsd, x1, g2, final_norm, seq_of_tile, tm):
    t, d = x1.shape
    e = wt.shape[1]
    ds_ = act.shape[1]
    kh = d // 2
    nt = t // tm
    row = lambda i, s: (i, 0)
    fixed = lambda i, s: (0, 0)
    y_specs = [pl.BlockSpec((tm, kh), functools.partial(lambda i, s, k: (k * nt + i, 0), k=k))
               for k in range(TOP_K)]
    return pl.pallas_call(
        functools.partial(_combine_kernel, top_k=TOP_K),
        out_shape=jax.ShapeDtypeStruct((t, d), F32),
        grid_spec=pltpu.PrefetchScalarGridSpec(
            num_scalar_prefetch=1, grid=(nt,),
            in_specs=y_specs + [pl.BlockSpec((tm, e), row),
                                pl.BlockSpec((tm, ds_), row),
                                pl.BlockSpec((ds_, d), fixed),
                                pl.BlockSpec((tm, d), row),
                                pl.BlockSpec((1, 1, d), lambda i, s: (s[i], 0, 0)),
                                pl.BlockSpec((1, d), fixed)],
            out_specs=pl.BlockSpec((tm, d), row)),
        compiler_params=_cparams(("arbitrary",)),
        name="combine",
    )(seq_of_tile, *([ys] * TOP_K), wt, act, wsd, x1, g2, final_norm.reshape(1, d))


def _seq_tables(seqs, tile):
    sid, first, last = [], [], []
    s = 0
    for n_seq, seq_len in seqs:
        per = seq_len // tile
        for _ in range(n_seq):
            sid += [s] * per
            first += [1] + [0] * (per - 1)
            last += [0] * (per - 1) + [1]
            s += 1
    as_i32 = lambda v: jnp.asarray(np.asarray(v, np.int32))
    return as_i32(sid), as_i32(first), as_i32(last)


def _layer(x, c_all, seqs, p):
    t, d = x.shape
    w_ssm = p["ssm_d"].shape[0]
    w_conv = p["conv_b"].shape[0]
    n_e = p["router_w"].shape[1]
    min_seq = min(sl for _, sl in seqs)

    tm = _tile(min_seq, 1024)
    tn_row = _tile(min_seq, 256)
    seq_mm, _, _ = _seq_tables(seqs, tm)
    seq_nr, _, _ = _seq_tables(seqs, tn_row)

    n_seq_total = c_all.shape[0]
    ada = _ada(c_all, p["w_ada"], p["b_ada"])
    sh1, sc1, g1, sh2, sc2, g2 = [a.reshape(n_seq_total, 1, d) for a in jnp.split(ada, 6, axis=1)]

    bf = lambda w: w.astype(BF16)
    w_in = p["w_in"]
    h = _norm1(x, p["norm_mix"], sc1, sh1, seq_nr, tn_row)

    u_ssm = _proj(h, bf(w_in[:, :w_ssm]), tm, _tile(w_ssm, 1024), out_dtype=F32, name="in_ssm")
    tabs = _ssm_tables(p["ssm_a_re"], p["ssm_a_im"], p["ssm_log_dt"], p["ssm_b_re"], p["ssm_b_im"],
                       p["ssm_c_re"], p["ssm_c_im"], p["ssm_d"])
    ya = _ssm_branch(u_ssm, tabs, seqs)
    wgl = p["w_ssm_glu"]
    a_act = _glu_proj(ya, bf(wgl[:, :w_ssm]), bf(wgl[:, w_ssm:]), tm, _tile(w_ssm, 1024), False, "ssm_glu")

    ub = _glu_proj(h, bf(w_in[:, w_ssm:w_ssm + w_conv]), bf(w_in[:, w_ssm + w_conv:]),
                   tm, _tile(w_conv, 512), False, "in_conv_glu")
    r_conv = _tile(min_seq, 256)
    _, first_c, last_c = _seq_tables(seqs, r_conv)
    cv = _conv_ln(ub, p["conv_w"], p["conv_b"], p["conv_norm_g"], p["conv_norm_b"], first_c, last_c, r_conv)

    wmg = p["w_merge_gate"]
    bmg = p["b_merge_gate"]
    tm_merge = _tile(min_seq, 512)
    m = _merge(h, a_act, cv, bf(wmg[:, :d]), bf(wmg[:, d:]), bf(p["w_branch_ssm"]), bf(p["w_branch_conv"]),
               bmg[:d].reshape(1, d), bmg[d:].reshape(1, d), tm_merge, _tile(d, 512))
    x1 = _outproj(m, bf(p["w_out"]), x, g1, seq_mm, tm, _tile(d, 512))

    h2, h2p, logits = _norm2(x1, p["norm_ffn"], sc2, sh2, p["router_w"], seq_nr, tn_row)
    tm_route = _tile(t, 256)
    top_idx, top_w, rank, counts = _route(logits, p["router_bias"], tm_route)
    top_idx = top_idx[:, :TOP_K]
    rank = rank[:, :TOP_K]

    r_blk = 256 if t * TOP_K >= 256 * n_e else 8
    counts = counts[0]
    padded = (counts + r_blk - 1) // r_blk * r_blk
    pad_end = jnp.cumsum(padded)
    pad_start = pad_end - padded
    n_blocks = -(-(t * TOP_K) // r_blk) + n_e
    onehot = top_idx[:, :, None] == jnp.arange(n_e, dtype=I32)[None, None, :]
    dest = jnp.sum(jnp.where(onehot, pad_start[None, None, :], 0), axis=-1) + rank
    pos = jnp.arange(n_blocks * r_blk, dtype=I32)
    dump = TOP_K * t + ((pos // r_blk) % 2) * r_blk + pos % r_blk
    out_rows = jnp.arange(TOP_K, dtype=I32)[None, :] * t + jnp.arange(t, dtype=I32)[:, None]
    row_dst = dump.at[dest.reshape(-1)].set(out_rows.reshape(-1))
    row_src = jnp.where(row_dst < TOP_K * t, row_dst % t, 0)
    row_idx = jnp.stack([row_src.reshape(n_blocks, r_blk), row_dst.reshape(n_blocks, r_blk)], axis=1)
    blk_start = jnp.arange(n_blocks, dtype=I32) * r_blk
    blk_e = jnp.minimum(jnp.searchsorted(pad_end, blk_start, side="right"), n_e - 1).astype(I32)
    nblk = (pad_end[-1] // r_blk).astype(I32).reshape(1)

    ys = _experts(blk_e, nblk, row_idx, h2p, bf(p["exp_w_gate"]), bf(p["exp_w_up"]), bf(p["exp_w_down"]),
                  TOP_K * t)

    act = _glu_proj(h2, bf(p["shared_w_gate"]), bf(p["shared_w_up"]), tm, p["shared_w_gate"].shape[1],
                    True, "shared_act")
    tm_c = _tile(min_seq, 128)
    seq_c, _, _ = _seq_tables(seqs, tm_c)
    return _combine(ys, top_w, act, bf(p["shared_w_down"]), x1, g2, p["final_norm"], seq_c, tm_c)


def kernel(x_prompt, x_sample, c_prompt, c_sample, w_ada, b_ada, norm_mix, w_in, ssm_a_re, ssm_a_im, ssm_log_dt, ssm_b_re, ssm_b_im, ssm_c_re, ssm_c_im, ssm_d, w_ssm_glu, w_branch_ssm, conv_w, conv_b, conv_norm_g, conv_norm_b, w_branch_conv, w_merge_gate, b_merge_gate, w_out, norm_ffn, router_w, router_bias, exp_w_gate, exp_w_up, exp_w_down, shared_w_gate, shared_w_up, shared_w_down, final_norm):
    assert w_ada.shape[0] == 1, "single-layer trunk"
    bp, lp, d = x_prompt.shape
    bs, ls, _ = x_sample.shape
    seqs = [(bp, lp), (bs, ls)]
    x = jnp.concatenate([x_prompt.reshape(bp * lp, d), x_sample.reshape(bs * ls, d)], axis=0)
    n_seq = bp + bs
    c_all = jnp.concatenate([c_prompt, c_sample, jnp.zeros((-n_seq % 8, d), F32)], axis=0)
    params = dict(
        w_ada=w_ada[0], b_ada=b_ada[0], norm_mix=norm_mix[0], w_in=w_in[0],
        ssm_a_re=ssm_a_re[0], ssm_a_im=ssm_a_im[0], ssm_log_dt=ssm_log_dt[0],
        ssm_b_re=ssm_b_re[0], ssm_b_im=ssm_b_im[0], ssm_c_re=ssm_c_re[0], ssm_c_im=ssm_c_im[0],
        ssm_d=ssm_d[0], w_ssm_glu=w_ssm_glu[0], w_branch_ssm=w_branch_ssm[0],
        conv_w=conv_w[0], conv_b=conv_b[0], conv_norm_g=conv_norm_g[0], conv_norm_b=conv_norm_b[0],
        w_branch_conv=w_branch_conv[0], w_merge_gate=w_merge_gate[0], b_merge_gate=b_merge_gate[0],
        w_out=w_out[0], norm_ffn=norm_ffn[0], router_w=router_w[0], router_bias=router_bias[0],
        exp_w_gate=exp_w_gate[0], exp_w_up=exp_w_up[0], exp_w_down=exp_w_down[0],
        shared_w_gate=shared_w_gate[0], shared_w_up=shared_w_up[0], shared_w_down=shared_w_down[0],
        final_norm=final_norm)
    y = _layer(x, c_all, seqs, params)
    return (y[:bp * lp].reshape(bp, lp, d), y[bp * lp:].reshape(bs, ls, d))
```

```python
import functools
import math

import jax
import jax.numpy as jnp
import numpy as np
from jax import lax
from jax.experimental import pallas as pl
from jax.experimental.pallas import tpu as pltpu

F32 = jnp.float32
BF16 = jnp.bfloat16
I32 = jnp.int32
U32 = jnp.uint32

SSM_GROUP_CH = 16
SSM_CHUNK = 16
N_ROUTE_GROUPS = 8
TOPK_ROUTE_GROUPS = 4
TOP_K = 8
ROUTED_SCALE = 2.5
RMS_EPS = 1e-6
LN_EPS = 1e-5

V7X_VMEM_LIMIT = 56 * 1024 * 1024
LANES = 128


def _cparams(sem, vmem=V7X_VMEM_LIMIT):
    return pltpu.CompilerParams(dimension_semantics=sem, vmem_limit_bytes=vmem)


def _tile(n, want):
    t = min(n, want)
    while n % t:
        t -= 1
    return t


def _sigmoid(x):
    return 1.0 / (1.0 + jnp.exp(-x))


def _bdot(a, b):
    return jnp.dot(a, b, preferred_element_type=F32)


def _ada_kernel(c_ref, w_ref, b_ref, o_ref):
    c = c_ref[...]
    s = (c * _sigmoid(c)).astype(BF16)
    o_ref[...] = _bdot(s, w_ref[...].astype(BF16)) + b_ref[...]


def _ada(c_all, w_ada, b_ada):
    nseq, d = c_all.shape
    n = w_ada.shape[1]
    tn = _tile(n, 512)
    return pl.pallas_call(
        _ada_kernel,
        out_shape=jax.ShapeDtypeStruct((nseq, n), F32),
        grid=(n // tn,),
        in_specs=[pl.BlockSpec((nseq, d), lambda j: (0, 0)),
                  pl.BlockSpec((d, tn), lambda j: (0, j)),
                  pl.BlockSpec((1, tn), lambda j: (0, j))],
        out_specs=pl.BlockSpec((nseq, tn), lambda j: (0, j)),
        compiler_params=_cparams(("arbitrary",)),
        name="ada",
    )(c_all, w_ada, b_ada.reshape(1, n))


def _pack_bf16_pair(lo, hi):
    lo_b = pltpu.bitcast(lo.astype(BF16).astype(F32), U32) >> 16
    hi_b = pltpu.bitcast(hi.astype(BF16).astype(F32), U32) & jnp.uint32(0xFFFF0000)
    return hi_b | lo_b


def _unpack_bf16_pair(w):
    lo = pltpu.bitcast(w << 16, F32)
    hi = pltpu.bitcast(w & jnp.uint32(0xFFFF0000), F32)
    return lo, hi


def _store_row_slabs(ref, val, pitch):
    rows, width = val.shape
    for j in range(width // LANES):
        ref[pl.ds(j, rows, stride=pitch), :] = val[:, j * LANES:(j + 1) * LANES]


def _load_row_slabs(ref, rows, n_tiles, pitch):
    return jnp.concatenate([ref[pl.ds(j, rows, stride=pitch), :] for j in range(n_tiles)], axis=1)


def _modulated_norm(x, g, sc, sh):
    ms = jnp.mean(x * x, axis=-1, keepdims=True)
    return x * lax.rsqrt(ms + RMS_EPS) * g * (1.0 + sc) + sh


def _two_source_specs(block, n_a, n_inner=0):
    if n_inner:
        first = lambda i, j, *_: (jnp.minimum(i, n_a - 1), jnp.where(i < n_a, j, 0))
        second = lambda i, j, *_: (jnp.maximum(i - n_a, 0), jnp.where(i < n_a, 0, j))
    else:
        first = lambda i, *_: (jnp.minimum(i, n_a - 1), 0)
        second = lambda i, *_: (jnp.maximum(i - n_a, 0), 0)
    return pl.BlockSpec(block, first), pl.BlockSpec(block, second)


def _pick_source(i, n_a, a_ref, b_ref):
    return jnp.where(i < n_a, a_ref[...], b_ref[...])


def _norm1_kernel(seq_ref, xa_ref, xb_ref, g_ref, sc_ref, sh_ref, o_ref, *, n_a):
    del seq_ref
    x = _pick_source(pl.program_id(0), n_a, xa_ref, xb_ref)
    o_ref[...] = _modulated_norm(x, g_ref[...], sc_ref[0], sh_ref[0]).astype(BF16)


def _norm2_kernel(seq_ref, x_ref, g_ref, sc_ref, sh_ref, rw_ref, h_ref, hp_ref, lg_ref):
    del seq_ref
    h = _modulated_norm(x_ref[...], g_ref[...], sc_ref[0], sh_ref[0])
    half = h.shape[1] // 2
    h_ref[...] = h.astype(BF16)
    _store_row_slabs(hp_ref, _pack_bf16_pair(h[:, :half], h[:, half:]), half // LANES)
    lg_ref[...] = jnp.dot(h, rw_ref[...], preferred_element_type=F32,
                          precision=lax.Precision.HIGHEST)


def _norm1(xa, xb, g, sc, sh, seq_of_tile, tm):
    d = xa.shape[1]
    t = xa.shape[0] + xb.shape[0]
    n_a = xa.shape[0] // tm
    row = lambda i, s: (i, 0)
    per_seq = lambda i, s: (s[i], 0, 0)
    return pl.pallas_call(
        functools.partial(_norm1_kernel, n_a=n_a),
        out_shape=jax.ShapeDtypeStruct((t, d), BF16),
        grid_spec=pltpu.PrefetchScalarGridSpec(
            num_scalar_prefetch=1, grid=(t // tm,),
            in_specs=[*_two_source_specs((tm, d), n_a),
                      pl.BlockSpec((1, d), lambda i, s: (0, 0)),
                      pl.BlockSpec((1, 1, d), per_seq),
                      pl.BlockSpec((1, 1, d), per_seq)],
            out_specs=pl.BlockSpec((tm, d), row)),
        compiler_params=_cparams(("arbitrary",)),
        name="norm1",
    )(seq_of_tile, xa, xb, g.reshape(1, d), sc, sh)


def _norm2(x, g, sc, sh, router_w, seq_of_tile, tm):
    t, d = x.shape
    e = router_w.shape[1]
    slab_rows = d // 2 // LANES
    row = lambda i, s: (i, 0)
    per_seq = lambda i, s: (s[i], 0, 0)
    return pl.pallas_call(
        _norm2_kernel,
        out_shape=(jax.ShapeDtypeStruct((t, d), BF16),
                   jax.ShapeDtypeStruct((t * slab_rows, LANES), U32),
                   jax.ShapeDtypeStruct((t, e), F32)),
        grid_spec=pltpu.PrefetchScalarGridSpec(
            num_scalar_prefetch=1, grid=(t // tm,),
            in_specs=[pl.BlockSpec((tm, d), row),
                      pl.BlockSpec((1, d), lambda i, s: (0, 0)),
                      pl.BlockSpec((1, 1, d), per_seq),
                      pl.BlockSpec((1, 1, d), per_seq),
                      pl.BlockSpec((d, e), lambda i, s: (0, 0))],
            out_specs=(pl.BlockSpec((tm, d), row),
                       pl.BlockSpec((tm * slab_rows, LANES), row),
                       pl.BlockSpec((tm, e), row))),
        compiler_params=_cparams(("arbitrary",)),
        name="norm2_router",
    )(seq_of_tile, x, g.reshape(1, d), sc, sh, router_w)


def _proj_kernel(x_ref, w_ref, o_ref):
    o_ref[...] = _bdot(x_ref[...], w_ref[...]).astype(o_ref.dtype)


def _proj(x, w, tm, tn, out_dtype=BF16, name="proj"):
    t, k = x.shape
    n = w.shape[1]
    return pl.pallas_call(
        _proj_kernel,
        out_shape=jax.ShapeDtypeStruct((t, n), out_dtype),
        grid=(t // tm, n // tn),
        in_specs=[pl.BlockSpec((tm, k), lambda i, j: (i, 0)),
                  pl.BlockSpec((k, tn), lambda i, j: (0, j))],
        out_specs=pl.BlockSpec((tm, tn), lambda i, j: (i, j)),
        compiler_params=_cparams(("parallel", "arbitrary")),
        name=name,
    )(x, w)


def _glu_kernel(*refs, silu_first, n_a):
    if n_a is None:
        x_ref, wa_ref, wb_ref, o_ref = refs
        x = x_ref[...].astype(BF16)
    else:
        xa_ref, xb_ref, wa_ref, wb_ref, o_ref = refs
        x = _pick_source(pl.program_id(0), n_a, xa_ref, xb_ref).astype(BF16)
    a = _bdot(x, wa_ref[...])
    b = _bdot(x, wb_ref[...])
    if silu_first:
        y = a * _sigmoid(a) * b
    else:
        y = a * _sigmoid(b)
    o_ref[...] = y.astype(o_ref.dtype)


def _glu_proj(x, wa, wb, tm, tn, silu_first, name, x_tail=None):
    k = x.shape[1]
    n = wa.shape[1]
    if x_tail is None:
        t, n_a = x.shape[0], None
        x_args, x_specs = [x], [pl.BlockSpec((tm, k), lambda i, j: (i, 0))]
    else:
        t, n_a = x.shape[0] + x_tail.shape[0], x.shape[0] // tm
        x_args, x_specs = [x, x_tail], list(_two_source_specs((tm, k), n_a))
    return pl.pallas_call(
        functools.partial(_glu_kernel, silu_first=silu_first, n_a=n_a),
        out_shape=jax.ShapeDtypeStruct((t, n), BF16),
        grid=(t // tm, n // tn),
        in_specs=x_specs + [pl.BlockSpec((k, tn), lambda i, j: (0, j)),
                            pl.BlockSpec((k, tn), lambda i, j: (0, j))],
        out_specs=pl.BlockSpec((tm, tn), lambda i, j: (i, j)),
        compiler_params=_cparams(("parallel", "arbitrary")),
        name=name,
    )(*x_args, wa, wb)


def _merge_kernel(h_ref, a_ref, c_ref, wga_ref, wgb_ref, wa_ref, wc_ref, ba_ref, bb_ref, o_ref):
    h = h_ref[...]
    ga = _sigmoid(_bdot(h, wga_ref[...]) + ba_ref[...])
    gb = _sigmoid(_bdot(h, wgb_ref[...]) + bb_ref[...])
    ya = _bdot(a_ref[...], wa_ref[...])
    yb = _bdot(c_ref[...], wc_ref[...])
    o_ref[...] = (ga * ya + gb * yb).astype(o_ref.dtype)


def _merge(h, a, cv, wga, wgb, wa, wc, ba, bb, tm, tn):
    t, d = h.shape
    ka = a.shape[1]
    kc = cv.shape[1]
    row = lambda i, j: (i, 0)
    col = lambda i, j: (0, j)
    return pl.pallas_call(
        _merge_kernel,
        out_shape=jax.ShapeDtypeStruct((t, d), BF16),
        grid=(t // tm, d // tn),
        in_specs=[pl.BlockSpec((tm, d), row), pl.BlockSpec((tm, ka), row), pl.BlockSpec((tm, kc), row),
                  pl.BlockSpec((d, tn), col), pl.BlockSpec((d, tn), col),
                  pl.BlockSpec((ka, tn), col), pl.BlockSpec((kc, tn), col),
                  pl.BlockSpec((1, tn), col), pl.BlockSpec((1, tn), col)],
        out_specs=pl.BlockSpec((tm, tn), lambda i, j: (i, j)),
        compiler_params=_cparams(("parallel", "arbitrary")),
        name="merge",
    )(h, a, cv, wga, wgb, wa, wc, ba, bb)


def _outproj_kernel(seq_ref, m_ref, w_ref, xa_ref, xb_ref, g_ref, o_ref, *, n_a):
    del seq_ref
    x = _pick_source(pl.program_id(0), n_a, xa_ref, xb_ref)
    o_ref[...] = x + g_ref[0] * _bdot(m_ref[...], w_ref[...])


def _outproj(m, w, xa, xb, gate, seq_of_tile, tm, tn):
    t, k = m.shape
    d = w.shape[1]
    n_a = xa.shape[0] // tm
    return pl.pallas_call(
        functools.partial(_outproj_kernel, n_a=n_a),
        out_shape=jax.ShapeDtypeStruct((t, d), F32),
        grid_spec=pltpu.PrefetchScalarGridSpec(
            num_scalar_prefetch=1, grid=(t // tm, d // tn),
            in_specs=[pl.BlockSpec((tm, k), lambda i, j, s: (i, 0)),
                      pl.BlockSpec((k, tn), lambda i, j, s: (0, j)),
                      *_two_source_specs((tm, tn), n_a, n_inner=1),
                      pl.BlockSpec((1, 1, tn), lambda i, j, s: (s[i], 0, j))],
            out_specs=pl.BlockSpec((tm, tn), lambda i, j, s: (i, j))),
        compiler_params=_cparams(("parallel", "arbitrary")),
        name="outproj",
    )(seq_of_tile, m, w, xa, xb, gate)


def _lagmat_kernel(c_ref, b_ref, o_ref):
    for g in range(c_ref.shape[0]):
        o_ref[g] = jnp.dot(c_ref[g], b_ref[g], preferred_element_type=F32,
                           precision=lax.Precision.HIGHEST)


def _lagmat(cs, bs):
    g2, c, n2 = cs.shape
    w = bs.shape[2]
    gb = _tile(g2, 8)
    return pl.pallas_call(
        _lagmat_kernel,
        out_shape=jax.ShapeDtypeStruct((g2, c, w), F32),
        grid=(g2 // gb,),
        in_specs=[pl.BlockSpec((gb, c, n2), lambda i: (i, 0, 0)),
                  pl.BlockSpec((gb, n2, w), lambda i: (i, 0, 0))],
        out_specs=pl.BlockSpec((gb, c, w), lambda i: (i, 0, 0)),
        compiler_params=_cparams(("arbitrary",)),
        name="ssm_lagmat",
    )(cs, bs)


def _ssm_tables(a_re, a_im, log_dt, b_re, b_im, c_re, c_im, d_skip):
    tc, ch = SSM_CHUNK, SSM_GROUP_CH
    _, g, n = a_re.shape
    dt = jnp.exp(log_dt)[:, :, None]
    zr, zi = a_re * dt, a_im * dt
    ks = jnp.arange(tc + 1, dtype=F32)[:, None, None, None]
    mag = jnp.exp(ks * zr[None])
    pw_re, pw_im = mag * jnp.cos(ks * zi[None]), mag * jnp.sin(ks * zi[None])
    lb_re, lb_im = pw_re[1], pw_im[1]
    den = a_re * a_re + a_im * a_im
    q_re = ((lb_re - 1.0) * a_re + lb_im * a_im) / den
    q_im = (lb_im * a_re - (lb_re - 1.0) * a_im) / den
    bb_re = q_re[..., None] * b_re - q_im[..., None] * b_im
    bb_im = q_re[..., None] * b_im + q_im[..., None] * b_re
    p_re, p_im = pw_re[:tc, ..., None], pw_im[:tc, ..., None]
    bk_re = p_re * bb_re[None] - p_im * bb_im[None]
    bk_im = p_re * bb_im[None] + p_im * bb_re[None]

    bs = jnp.concatenate([bk_re, bk_im], axis=3)
    bs = bs.transpose(1, 2, 3, 0, 4).reshape(2 * g, 2 * n, tc * ch)
    cs = jnp.concatenate([c_re, -c_im], axis=2)
    cs = jnp.broadcast_to(cs[None], (2, g, ch, 2 * n)).reshape(2 * g, ch, 2 * n)
    m = _lagmat(cs, bs).reshape(2, g, ch, tc, ch)

    s_idx = np.arange(tc)[:, None]
    t_idx = np.arange(tc)[None, :]
    lag_f = np.clip(t_idx - s_idx, 0, tc - 1)
    lag_b = np.clip(s_idx - t_idx, 0, tc - 1)
    mf = m[0][:, :, lag_f, :] * jnp.asarray(t_idx >= s_idx, F32)[None, None, :, :, None]
    mb = m[1][:, :, lag_b, :] * jnp.asarray(s_idx >= t_idx, F32)[None, None, :, :, None]
    toep = (mf + mb).transpose(0, 2, 4, 3, 1).reshape(g, tc * ch, tc * ch)

    def rows_sc(x):
        return x.transpose(1, 0, 3, 2).reshape(g, tc * ch, n)
    pin = jnp.concatenate([rows_sc(bk_re[::-1, 0]), rows_sc(bk_re[:, 1]),
                           rows_sc(bk_im[::-1, 0]), rows_sc(bk_im[:, 1])], axis=2)

    def w_tab(pr, pi):
        wr = c_re[None] * pr[:, :, None, :] - c_im[None] * pi[:, :, None, :]
        wi = c_re[None] * pi[:, :, None, :] + c_im[None] * pr[:, :, None, :]
        to_rows = lambda x: x.transpose(1, 3, 0, 2).reshape(g, n, tc * ch)
        return to_rows(wr), to_rows(-wi)
    qf_re, qf_im = w_tab(pw_re[1:tc + 1, 0], pw_im[1:tc + 1, 0])
    qb_re, qb_im = w_tab(pw_re[tc:0:-1, 1], pw_im[tc:0:-1, 1])
    qout = jnp.concatenate([qf_re, qb_re, qf_im, qb_im], axis=1)

    a_step_re = jnp.concatenate([pw_re[tc, 0], pw_re[tc, 1]], axis=1)
    a_step_im = jnp.concatenate([pw_im[tc, 0], pw_im[tc, 1]], axis=1)
    d_tile = jnp.tile(d_skip.reshape(g, 1, ch), (1, tc, 1)).reshape(g, tc * ch)
    return (toep.astype(BF16), pin.astype(BF16), qout.astype(BF16),
            a_step_re, a_step_im, d_tile)


def _gelu_tanh(x):
    return 0.5 * x * (1.0 + jnp.tanh(math.sqrt(2.0 / math.pi) * (x + 0.044715 * (x * x * x))))


def _ssm_kernel(u_ref, t_ref, p_ref, q_ref, are_ref, aim_ref, d_ref, o_ref,
                ug_ref, yg_ref, sin_re, sin_im, hf_re, hf_im, hb_re, hb_im):
    gb, w, _ = t_ref.shape
    tc, ch = SSM_CHUNK, SSM_GROUP_CH
    nc = u_ref.shape[0] // tc
    half = w // 2
    n_col = w // LANES
    slots = LANES // ch
    slot = lax.broadcasted_iota(I32, (8, LANES), 1) // ch
    in_slot = [slot == k for k in range(slots)]

    def to_chunks(rc, _):
        xs = [u_ref[pl.ds(rc * (8 * tc) + t, 8, stride=tc), :] for t in range(tc)]
        cols = [[jnp.zeros((8, LANES), F32) for _ in range(n_col)] for _ in range(gb)]
        for col in range(n_col):
            for s in range(slots):
                v = jnp.zeros((8, LANES), F32)
                for ts in range(slots):
                    v = jnp.where(in_slot[(ts - s) % slots], xs[col * slots + ts], v)
                vr = v if s == 0 else pltpu.roll(v, s * ch, 1)
                for ts in range(slots):
                    g = (ts - s) % slots
                    cols[g][col] = jnp.where(in_slot[ts], vr, cols[g][col])
        for g in range(gb):
            ug_ref[g, pl.ds(rc * 8, 8), :] = jnp.concatenate(cols[g], axis=1)
        return 0

    lax.fori_loop(0, nc // 8, to_chunks, 0)

    for g in range(gb):
        s_in = _bdot(ug_ref[g].astype(BF16), p_ref[g])
        sin_re[pl.ds(g, nc, stride=gb), :] = s_in[:, :half]
        sin_im[pl.ds(g, nc, stride=gb), :] = s_in[:, half:]

    a_re = are_ref[...]
    a_im = aim_ref[...]
    is_fwd = lax.broadcasted_iota(I32, (gb, half), 1) < (half // 2)

    def step(j, carry):
        s_re, s_im = carry
        at_f = pl.ds(pl.multiple_of(j * gb, gb), gb)
        at_b = pl.ds(pl.multiple_of((nc - 1 - j) * gb, gb), gb)
        hf_re[at_f, :] = s_re
        hf_im[at_f, :] = s_im
        hb_re[at_b, :] = s_re
        hb_im[at_b, :] = s_im
        in_re = jnp.where(is_fwd, sin_re[at_f, :], sin_re[at_b, :])
        in_im = jnp.where(is_fwd, sin_im[at_f, :], sin_im[at_b, :])
        return (a_re * s_re - a_im * s_im + in_re, a_re * s_im + a_im * s_re + in_im)

    zero = jnp.zeros((gb, half), F32)
    lax.fori_loop(0, nc, step, (zero, zero))

    fwd_lane = lax.broadcasted_iota(I32, (nc, half), 1) < (half // 2)
    for g in range(gb):
        u = ug_ref[g]
        rows = pl.ds(g, nc, stride=gb)
        hin = jnp.concatenate([jnp.where(fwd_lane, hf_re[rows, :], hb_re[rows, :]),
                               jnp.where(fwd_lane, hf_im[rows, :], hb_im[rows, :])], axis=1)
        y = _bdot(u.astype(BF16), t_ref[g]) + _bdot(hin.astype(BF16), q_ref[g]) + d_ref[pl.ds(g, 1), :] * u
        yg_ref[g] = _gelu_tanh(y)

    def from_chunks(rc, _):
        for col in range(n_col):
            ys = [yg_ref[g, pl.ds(rc * 8, 8), pl.ds(col * LANES, LANES)] for g in range(gb)]
            z = [jnp.zeros((8, LANES), F32) for _ in range(slots)]
            for s in range(slots):
                v = jnp.zeros((8, LANES), F32)
                for g in range(gb):
                    v = jnp.where(in_slot[(g - s) % slots], ys[g], v)
                vr = v if s == 0 else pltpu.roll(v, s * ch, 1)
                for g in range(gb):
                    ts = (g - s) % slots
                    z[ts] = jnp.where(in_slot[g], vr, z[ts])
            for ts in range(slots):
                o_ref[pl.ds(rc * (8 * tc) + col * slots + ts, 8, stride=tc), :] = z[ts]
        return 0

    lax.fori_loop(0, nc // 8, from_chunks, 0)


def _ssm_seqs(u, tabs, row0, n_seq, seq_len):
    toep, pin, qout, a_re, a_im, d_tile = tabs
    _, wch = u.shape
    w = toep.shape[1]
    half = w // 2
    gb = LANES // SSM_GROUP_CH
    nc = seq_len // SSM_CHUNK
    assert nc % 8 == 0 and row0 % seq_len == 0 and wch % LANES == 0
    blk0 = row0 // seq_len
    tab_spec = pl.BlockSpec((gb, w, w), lambda i, b: (i, 0, 0))
    vec_spec = lambda width: pl.BlockSpec((gb, width), lambda i, b: (i, 0))
    return pl.pallas_call(
        _ssm_kernel,
        out_shape=jax.ShapeDtypeStruct((n_seq * seq_len, wch), F32),
        grid=(wch // LANES, n_seq),
        in_specs=[pl.BlockSpec((seq_len, LANES), lambda i, b: (blk0 + b, i)),
                  tab_spec, tab_spec, tab_spec,
                  vec_spec(half), vec_spec(half), vec_spec(w)],
        out_specs=pl.BlockSpec((seq_len, LANES), lambda i, b: (b, i)),
        scratch_shapes=[pltpu.VMEM((gb, nc, w), F32)] * 2 + [pltpu.VMEM((gb * nc, half), F32)] * 6,
        compiler_params=_cparams(("parallel", "arbitrary")),
        name="ssm",
    )(u, toep, pin, qout, a_re, a_im, d_tile)


def _ssm_branch(u, tabs, seqs):
    outs = []
    row0 = 0
    for n_seq, seq_len in seqs:
        outs.append(_ssm_seqs(u, tabs, row0, n_seq, seq_len))
        row0 += n_seq * seq_len
    return outs


def _conv_kernel(first_ref, last_ref, prev_ref, cur_ref, next_ref, w_ref, b_ref, g_ref, beta_ref,
                 o_ref, xs_ref, acc_ref, *, halo, row_chunk):
    i = pl.program_id(0)
    r, c = cur_ref.shape
    kw = w_ref.shape[0]
    half = kw // 2
    n_shift = xs_ref.shape[0]
    keep_prev = jnp.where(first_ref[i] == 0, 1.0, 0.0)
    keep_next = jnp.where(last_ref[i] == 0, 1.0, 0.0)
    xs_ref[0, pl.ds(0, halo), :] = prev_ref[...].astype(F32) * keep_prev
    xs_ref[0, pl.ds(halo, r), :] = cur_ref[...].astype(F32)
    xs_ref[0, pl.ds(halo + r, halo), :] = next_ref[...].astype(F32) * keep_next
    span = r + 2 * halo - n_shift
    for b in range(1, n_shift):
        xs_ref[b, pl.ds(0, span), :] = xs_ref[0, pl.ds(b, span), :]

    col_chunk = _tile(c, 4 * LANES)
    n_cc = c // col_chunk

    def chunk(it, _):
        r0 = pl.multiple_of((it // n_cc) * row_chunk, row_chunk)
        c0 = pl.multiple_of((it % n_cc) * col_chunk, col_chunk)
        cols = pl.ds(c0, col_chunk)
        acc = jnp.zeros((row_chunk, col_chunk), F32)
        for k in range(kw):
            off = halo - half + k
            b = off % n_shift
            acc = acc + w_ref[pl.ds(k, 1), cols] * xs_ref[b, pl.ds(r0 + (off - b), row_chunk), cols]
        acc_ref[pl.ds(r0, row_chunk), cols] = acc
        return 0

    lax.fori_loop(0, (r // row_chunk) * n_cc, chunk, 0)
    ln_rows = _tile(r, 16)

    def norm_rows(it, _):
        rows = pl.ds(pl.multiple_of(it * ln_rows, ln_rows), ln_rows)
        y = acc_ref[rows, :] + b_ref[...]
        mu = jnp.mean(y, axis=-1, keepdims=True)
        yc = y - mu
        var = jnp.mean(yc * yc, axis=-1, keepdims=True)
        z = yc * lax.rsqrt(var + LN_EPS) * g_ref[...] + beta_ref[...]
        o_ref[rows, :] = (z * _sigmoid(z)).astype(o_ref.dtype)
        return 0

    lax.fori_loop(0, r // ln_rows, norm_rows, 0, unroll=2)


def _conv_ln(ub, conv_w, conv_b, ln_g, ln_b, first_of_tile, last_of_tile, r):
    t, c = ub.shape
    kw = conv_w.shape[0]
    halo = 16
    assert kw // 2 <= halo and r % halo == 0
    nb = r // halo
    n_halo_blocks = t // halo
    row_chunk = _tile(r, 32)
    vec = lambda i, f, l: (0, 0)
    return pl.pallas_call(
        functools.partial(_conv_kernel, halo=halo, row_chunk=row_chunk),
        out_shape=jax.ShapeDtypeStruct((t, c), BF16),
        grid_spec=pltpu.PrefetchScalarGridSpec(
            num_scalar_prefetch=2, grid=(t // r,),
            in_specs=[pl.BlockSpec((halo, c), lambda i, f, l: (jnp.maximum(i * nb - 1, 0), 0)),
                      pl.BlockSpec((r, c), lambda i, f, l: (i, 0)),
                      pl.BlockSpec((halo, c), lambda i, f, l: (jnp.minimum((i + 1) * nb, n_halo_blocks - 1), 0)),
                      pl.BlockSpec((kw, c), vec), pl.BlockSpec((1, c), vec),
                      pl.BlockSpec((1, c), vec), pl.BlockSpec((1, c), vec)],
            out_specs=pl.BlockSpec((r, c), lambda i, f, l: (i, 0)),
            scratch_shapes=[pltpu.VMEM((8, r + 2 * halo, c), F32), pltpu.VMEM((r, c), F32)]),
        compiler_params=_cparams(("arbitrary",)),
        name="conv_ln",
    )(first_of_tile, last_of_tile, ub, ub, ub, conv_w, conv_b.reshape(1, c),
      ln_g.reshape(1, c), ln_b.reshape(1, c))


def _seg_allreduce(x, lane, seg, op):
    n = x.shape[-1]
    s = 1
    while s < seg:
        up = pltpu.roll(x, n - s, 1)
        dn = pltpu.roll(x, s, 1)
        x = op(x, jnp.where((lane & s) == 0, up, dn))
        s *= 2
    return x


def _route_kernel(lg_ref, bias_ref, idx_ref, wt_ref, rank_ref, cnt_ref, carry_ref, *, n_groups, topk_groups, top_k):
    i = pl.program_id(0)
    tm, e = lg_ref.shape
    seg = e // n_groups
    lane = lax.broadcasted_iota(I32, (tm, e), 1)
    lane_f = lane.astype(F32)
    neg = jnp.float32(-jnp.inf)

    scores = _sigmoid(lg_ref[...])
    choice = scores + bias_ref[...]

    m1 = _seg_allreduce(choice, lane, seg, jnp.maximum)
    first = _seg_allreduce(jnp.where(choice == m1, lane_f, float(e)), lane, seg, jnp.minimum)
    m2 = _seg_allreduce(jnp.where(lane_f == first, neg, choice), lane, seg, jnp.maximum)
    gscore = m1 + m2

    beaten = jnp.zeros((tm, e), F32)
    for dshift in range(1, n_groups):
        other = pltpu.roll(gscore, dshift * seg, 1)
        other_is_lower = lane >= dshift * seg
        beaten = beaten + jnp.where(other_is_lower, jnp.where(other >= gscore, 1.0, 0.0),
                                    jnp.where(other > gscore, 1.0, 0.0))
    masked = jnp.where(beaten < topk_groups, choice, neg)

    idx_out = jnp.zeros((tm, e), F32)
    sc_out = jnp.zeros((tm, e), F32)
    sel = jnp.zeros((tm, e), F32)
    picks = []
    for k in range(top_k):
        m = jnp.max(masked, axis=1, keepdims=True)
        pick = jnp.min(jnp.where(masked == m, lane_f, float(e)), axis=1, keepdims=True)
        hit = lane_f == pick
        s_k = jnp.sum(jnp.where(hit, scores, 0.0), axis=1, keepdims=True)
        idx_out = jnp.where(lane == k, pick, idx_out)
        sc_out = jnp.where(lane == k, s_k, sc_out)
        sel = jnp.where(hit, 1.0, sel)
        masked = jnp.where(hit, neg, masked)
        picks.append(hit)
    denom = jnp.sum(sc_out, axis=1, keepdims=True)
    idx_ref[...] = idx_out.astype(I32)
    wt_ref[...] = sc_out / denom * ROUTED_SCALE

    @pl.when(i == 0)
    def _():
        carry_ref[...] = jnp.zeros_like(carry_ref)

    rows = lax.broadcasted_iota(I32, (tm, tm), 0)
    cols = lax.broadcasted_iota(I32, (tm, tm), 1)
    tri = jnp.where(cols < rows, 1.0, 0.0).astype(BF16)
    rank = _bdot(tri, sel.astype(BF16)) + carry_ref[...]
    rank_out = jnp.zeros((tm, e), F32)
    for k in range(top_k):
        r_k = jnp.sum(jnp.where(picks[k], rank, 0.0), axis=1, keepdims=True)
        rank_out = jnp.where(lane == k, r_k, rank_out)
    rank_ref[...] = rank_out.astype(I32)
    carry_ref[...] = carry_ref[...] + jnp.sum(sel, axis=0, keepdims=True)
    cnt_ref[...] = carry_ref[...].astype(I32)


def _route(logits, bias, tm):
    t, e = logits.shape
    row = lambda i: (i, 0)
    fixed = lambda i: (0, 0)
    return pl.pallas_call(
        functools.partial(_route_kernel, n_groups=N_ROUTE_GROUPS, topk_groups=TOPK_ROUTE_GROUPS, top_k=TOP_K),
        out_shape=(jax.ShapeDtypeStruct((t, e), I32), jax.ShapeDtypeStruct((t, e), F32),
                   jax.ShapeDtypeStruct((t, e), I32), jax.ShapeDtypeStruct((1, e), I32)),
        grid=(t // tm,),
        in_specs=[pl.BlockSpec((tm, e), row), pl.BlockSpec((1, e), fixed)],
        out_specs=(pl.BlockSpec((tm, e), row), pl.BlockSpec((tm, e), row),
                   pl.BlockSpec((tm, e), row), pl.BlockSpec((1, e), fixed)),
        scratch_shapes=[pltpu.VMEM((1, e), F32)],
        compiler_params=_cparams(("arbitrary",)),
        name="route",
    )(logits, bias.reshape(1, e))


def _dest_kernel(idx_ref, rank_ref, start_ref, o_ref, *, top_k):
    tm, e = idx_ref.shape
    lane = lax.broadcasted_iota(I32, (tm, e), 1)
    idx = idx_ref[...]
    start = start_ref[...].astype(F32)
    out = jnp.zeros((tm, e), F32)
    for k in range(top_k):
        s_k = jnp.sum(jnp.where(lane == idx[:, k:k + 1], start, 0.0), axis=1, keepdims=True)
        out = jnp.where(lane == k, s_k, out)
    o_ref[...] = out.astype(I32) + rank_ref[...]


def _dest_rows(top_idx, rank, pad_start, tm):
    t, e = top_idx.shape
    row = lambda i: (i, 0)
    return pl.pallas_call(
        functools.partial(_dest_kernel, top_k=TOP_K),
        out_shape=jax.ShapeDtypeStruct((t, e), I32),
        grid=(t // tm,),
        in_specs=[pl.BlockSpec((tm, e), row), pl.BlockSpec((tm, e), row), pl.BlockSpec((1, e), lambda i: (0, 0))],
        out_specs=pl.BlockSpec((tm, e), row),
        compiler_params=_cparams(("arbitrary",)),
        name="dest_rows",
    )(top_idx, rank, pad_start.reshape(1, e))


def _expert_kernel(blk_e_ref, nblk_ref, idx_hbm, x_hbm, wg_ref, wu_ref, wd_ref, ys_hbm,
                   idx_smem, xb0, xb1, ob0, ob1, sem_idx, sem_g, sem_s):
    del blk_e_ref
    b = pl.program_id(0)
    n = nblk_ref[0]
    kh = wg_ref.shape[1] // 2
    ns = kh // LANES
    pitch = ns + 1
    r = xb0.shape[0] // pitch
    xbufs = (xb0, xb1)
    obufs = (ob0, ob1)
    dump0 = ys_hbm.shape[0] // ns - 2 * r

    def idx_copy(blk, s):
        return pltpu.make_async_copy(idx_hbm.at[blk], idx_smem.at[s], sem_idx.at[s])

    def hbm_row(ref, row):
        return ref.at[pl.ds(pl.multiple_of(row * ns, ns), ns)]

    def vmem_row(ref, i):
        return ref.at[pl.ds(i * pitch, ns)]

    def gather_start(s):
        for i in range(r):
            pltpu.make_async_copy(hbm_row(x_hbm, idx_smem[s, 0, i]), vmem_row(xbufs[s], i), sem_g.at[s]).start()

    def gather_wait(s):
        for i in range(r):
            pltpu.make_async_copy(hbm_row(x_hbm, 0), vmem_row(xbufs[s], i), sem_g.at[s]).wait()

    def scatter_start(s):
        for i in range(r):
            pltpu.make_async_copy(vmem_row(obufs[s], i), hbm_row(ys_hbm, idx_smem[s, 1, i]), sem_s.at[s]).start()

    def scatter_wait(s):
        for i in range(r):
            pltpu.make_async_copy(vmem_row(obufs[s], i), hbm_row(ys_hbm, 0), sem_s.at[s]).wait()

    @pl.when(b == 0)
    def _():
        ob1[...] = jnp.zeros_like(ob1)
        for h in range(2):
            fill = pltpu.make_async_copy(ob1.at[pl.ds(0, r * ns)],
                                         ys_hbm.at[pl.ds((dump0 + h * r) * ns, r * ns)], sem_s.at[1])
            fill.start()
            fill.wait()
        first = idx_copy(0, 0)
        first.start()
        first.wait()
        gather_start(0)

        @pl.when(n > 1)
        def _():
            idx_copy(1, 1).start()

    def step(cur):
        nxt = 1 - cur
        gather_wait(cur)

        @pl.when(b + 1 < n)
        def _():
            idx_copy(b + 1, nxt).wait()
            gather_start(nxt)

        @pl.when(b >= 2)
        def _():
            scatter_wait(cur)

        lo, hi = _unpack_bf16_pair(_load_row_slabs(xbufs[cur], r, ns, pitch))
        lo = lo.astype(BF16)
        hi = hi.astype(BF16)
        gate = _bdot(lo, wg_ref[0, pl.ds(0, kh), :]) + _bdot(hi, wg_ref[0, pl.ds(kh, kh), :])
        up = _bdot(lo, wu_ref[0, pl.ds(0, kh), :]) + _bdot(hi, wu_ref[0, pl.ds(kh, kh), :])
        act = (gate * _sigmoid(gate) * up).astype(BF16)
        out = _bdot(act, wd_ref[0])
        _store_row_slabs(obufs[cur], _pack_bf16_pair(out[:, :kh], out[:, kh:]), pitch)
        scatter_start(cur)

        @pl.when(b + 2 < n)
        def _():
            idx_copy(b + 2, cur).start()

        @pl.when(b == n - 1)
        def _():
            scatter_wait(cur)

            @pl.when(b >= 1)
            def _():
                scatter_wait(nxt)

    for parity in range(2):
        @pl.when((b < n) & ((b & 1) == parity))
        def _(parity=parity):
            step(parity)


def _experts(blk_e, nblk, row_idx, xp, wg, wu, wd, n_out_rows):
    n_blocks, _, r = row_idx.shape
    n_e, d, de = wg.shape
    ns = d // 2 // LANES
    last = lambda b, be, nb: jnp.minimum(b, nb[0] - 1)
    return pl.pallas_call(
        _expert_kernel,
        out_shape=jax.ShapeDtypeStruct(((n_out_rows + 2 * r) * ns, LANES), U32),
        grid_spec=pltpu.PrefetchScalarGridSpec(
            num_scalar_prefetch=2, grid=(n_blocks,),
            in_specs=[pl.BlockSpec(memory_space=pl.ANY),
                      pl.BlockSpec(memory_space=pl.ANY),
                      pl.BlockSpec((1, d, de), lambda b, be, nb: (be[last(b, be, nb)], 0, 0)),
                      pl.BlockSpec((1, d, de), lambda b, be, nb: (be[last(b, be, nb)], 0, 0)),
                      pl.BlockSpec((1, de, d), lambda b, be, nb: (be[last(b, be, nb)], 0, 0))],
            out_specs=pl.BlockSpec(memory_space=pl.ANY),
            scratch_shapes=[pltpu.SMEM((2, 2, r), I32)] + [pltpu.VMEM((r * (ns + 1), LANES), U32)] * 4
                           + [pltpu.SemaphoreType.DMA((2,))] * 3),
        compiler_params=_cparams(("arbitrary",)),
        name="experts",
    )(blk_e, nblk, row_idx, xp, wg, wu, wd)


def _combine_kernel(seq_ref, *refs, top_k, n_a):
    del seq_ref
    y_refs = refs[:top_k]
    wt_ref, act_ref, wsd_ref, x_ref, g2_ref, fn_ref, oa_ref, ob_ref, routed_ref = refs[top_k:]
    i = pl.program_id(0)
    tm, d = x_ref.shape
    kh = d // 2
    ns = kh // LANES
    wt = wt_ref[...]
    w_cols = [wt[:, k:k + 1] for k in range(top_k)]
    for j in range(ns):
        r_lo = jnp.zeros((tm, LANES), F32)
        r_hi = jnp.zeros((tm, LANES), F32)
        for k in range(top_k):
            lo, hi = _unpack_bf16_pair(y_refs[k][pl.ds(j, tm, stride=ns), :])
            r_lo = r_lo + w_cols[k] * lo
            r_hi = r_hi + w_cols[k] * hi
        routed_ref[:, pl.ds(j * LANES, LANES)] = r_lo
        routed_ref[:, pl.ds(kh + j * LANES, LANES)] = r_hi
    shared = _bdot(act_ref[...], wsd_ref[...])
    y = x_ref[...] + g2_ref[0] * (shared + routed_ref[...])
    ms = jnp.mean(y * y, axis=-1, keepdims=True)
    out = y * lax.rsqrt(ms + RMS_EPS) * fn_ref[...]

    @pl.when(i < n_a)
    def _():
        oa_ref[...] = out

    @pl.when(i >= n_a)
    def _():
        ob_ref[...] = out


def _combine(ys, wt, act, wsd, x1, g2, final_norm, seq_of_tile, tm, rows_a):
    t, d = x1.shape
    n_a = rows_a // tm
    e = wt.shape[1]
    ds_ = act.shape[1]
    ns = d // 2 // LANES
    nt = t // tm
    row = lambda i, s: (i, 0)
    fixed = lambda i, s: (0, 0)
    y_specs = [pl.BlockSpec((tm * ns, LANES), functools.partial(lambda i, s, k: (k * nt + i, 0), k=k))
               for k in range(TOP_K)]
    return pl.pallas_call(
        functools.partial(_combine_kernel, top_k=TOP_K, n_a=n_a),
        out_shape=(jax.ShapeDtypeStruct((rows_a, d), F32), jax.ShapeDtypeStruct((t - rows_a, d), F32)),
        grid_spec=pltpu.PrefetchScalarGridSpec(
            num_scalar_prefetch=1, grid=(nt,),
            in_specs=y_specs + [pl.BlockSpec((tm, e), row),
                                pl.BlockSpec((tm, ds_), row),
                                pl.BlockSpec((ds_, d), fixed),
                                pl.BlockSpec((tm, d), row),
                                pl.BlockSpec((1, 1, d), lambda i, s: (s[i], 0, 0)),
                                pl.BlockSpec((1, d), fixed)],
            out_specs=_two_source_specs((tm, d), n_a),
            scratch_shapes=[pltpu.VMEM((tm, d), F32)]),
        compiler_params=_cparams(("arbitrary",)),
        name="combine",
    )(seq_of_tile, *([ys] * TOP_K), wt, act, wsd, x1, g2, final_norm.reshape(1, d))


def _seq_tables(seqs, tile):
    sid, first, last = [], [], []
    s = 0
    for n_seq, seq_len in seqs:
        per = seq_len // tile
        for _ in range(n_seq):
            sid += [s] * per
            first += [1] + [0] * (per - 1)
            last += [0] * (per - 1) + [1]
            s += 1
    as_i32 = lambda v: jnp.asarray(np.asarray(v, np.int32))
    return as_i32(sid), as_i32(first), as_i32(last)


def _layer(xa, xb, c_all, seqs, p):
    d = xa.shape[1]
    t = xa.shape[0] + xb.shape[0]
    w_ssm = p["ssm_d"].shape[0]
    w_conv = p["conv_b"].shape[0]
    n_e = p["router_w"].shape[1]
    min_seq = min(sl for _, sl in seqs)

    tm = _tile(min_seq, 1024)
    tn_row = _tile(min_seq, 256)
    seq_mm, _, _ = _seq_tables(seqs, tm)
    seq_nr, _, _ = _seq_tables(seqs, tn_row)

    n_seq_total = c_all.shape[0]
    ada = _ada(c_all, p["w_ada"], p["b_ada"])
    sh1, sc1, g1, sh2, sc2, g2 = [a.reshape(n_seq_total, 1, d) for a in jnp.split(ada, 6, axis=1)]

    bf = lambda w: w.astype(BF16)
    w_in = p["w_in"]
    h = _norm1(xa, xb, p["norm_mix"], sc1, sh1, seq_nr, tn_row)

    u_ssm = _proj(h, bf(w_in[:, :w_ssm]), tm, _tile(w_ssm, 1024), out_dtype=F32, name="in_ssm")
    tabs = _ssm_tables(p["ssm_a_re"], p["ssm_a_im"], p["ssm_log_dt"], p["ssm_b_re"], p["ssm_b_im"],
                       p["ssm_c_re"], p["ssm_c_im"], p["ssm_d"])
    ya_a, ya_b = _ssm_branch(u_ssm, tabs, seqs)
    wgl = p["w_ssm_glu"]
    a_act = _glu_proj(ya_a, bf(wgl[:, :w_ssm]), bf(wgl[:, w_ssm:]), _tile(min_seq, 512), _tile(w_ssm, 1024), False,
                      "ssm_glu", x_tail=ya_b)

    ub = _glu_proj(h, bf(w_in[:, w_ssm:w_ssm + w_conv]), bf(w_in[:, w_ssm + w_conv:]),
                   tm, _tile(w_conv, 512), False, "in_conv_glu")
    r_conv = _tile(min_seq, 256)
    _, first_c, last_c = _seq_tables(seqs, r_conv)
    cv = _conv_ln(ub, p["conv_w"], p["conv_b"], p["conv_norm_g"], p["conv_norm_b"], first_c, last_c, r_conv)

    wmg = p["w_merge_gate"]
    bmg = p["b_merge_gate"]
    tm_merge = _tile(min_seq, 512)
    m = _merge(h, a_act, cv, bf(wmg[:, :d]), bf(wmg[:, d:]), bf(p["w_branch_ssm"]), bf(p["w_branch_conv"]),
               bmg[:d].reshape(1, d), bmg[d:].reshape(1, d), tm_merge, _tile(d, 512))
    x1 = _outproj(m, bf(p["w_out"]), xa, xb, g1, seq_mm, tm, _tile(d, 512))

    h2, h2p, logits = _norm2(x1, p["norm_ffn"], sc2, sh2, p["router_w"], seq_nr, tn_row)
    tm_route = _tile(t, 256)
    top_idx, top_w, rank, counts = _route(logits, p["router_bias"], tm_route)

    r_blk = 256 if t * TOP_K >= 256 * n_e else 8
    counts = counts[0]
    padded = (counts + r_blk - 1) // r_blk * r_blk
    pad_end = jnp.cumsum(padded)
    pad_start = pad_end - padded
    n_blocks = -(-(t * TOP_K) // r_blk) + n_e
    dest = _dest_rows(top_idx, rank, pad_start, tm_route)[:, :TOP_K]
    pos = jnp.arange(n_blocks * r_blk, dtype=I32)
    dump = TOP_K * t + ((pos // r_blk) % 2) * r_blk + pos % r_blk
    out_rows = jnp.arange(TOP_K, dtype=I32)[None, :] * t + jnp.arange(t, dtype=I32)[:, None]
    row_dst = dump.at[dest.reshape(-1)].set(out_rows.reshape(-1))
    row_src = jnp.where(row_dst < TOP_K * t, row_dst % t, 0)
    row_idx = jnp.stack([row_src.reshape(n_blocks, r_blk), row_dst.reshape(n_blocks, r_blk)], axis=1)
    blk_start = jnp.arange(n_blocks, dtype=I32) * r_blk
    blk_e = jnp.minimum(jnp.searchsorted(pad_end, blk_start, side="right"), n_e - 1).astype(I32)
    nblk = (pad_end[-1] // r_blk).astype(I32).reshape(1)

    ys = _experts(blk_e, nblk, row_idx, h2p, bf(p["exp_w_gate"]), bf(p["exp_w_up"]), bf(p["exp_w_down"]),
                  TOP_K * t)

    act = _glu_proj(h2, bf(p["shared_w_gate"]), bf(p["shared_w_up"]), tm, p["shared_w_gate"].shape[1],
                    True, "shared_act")
    tm_c = _tile(min_seq, 128)
    seq_c, _, _ = _seq_tables(seqs, tm_c)
    return _combine(ys, top_w, act, bf(p["shared_w_down"]), x1, g2, p["final_norm"], seq_c, tm_c, xa.shape[0])


def kernel(x_prompt, x_sample, c_prompt, c_sample, w_ada, b_ada, norm_mix, w_in, ssm_a_re, ssm_a_im, ssm_log_dt, ssm_b_re, ssm_b_im, ssm_c_re, ssm_c_im, ssm_d, w_ssm_glu, w_branch_ssm, conv_w, conv_b, conv_norm_g, conv_norm_b, w_branch_conv, w_merge_gate, b_merge_gate, w_out, norm_ffn, router_w, router_bias, exp_w_gate, exp_w_up, exp_w_down, shared_w_gate, shared_w_up, shared_w_down, final_norm):
    assert w_ada.shape[0] == 1, "single-layer trunk"
    bp, lp, d = x_prompt.shape
    bs, ls, _ = x_sample.shape
    seqs = [(bp, lp), (bs, ls)]
    n_seq = bp + bs
    c_all = jnp.concatenate([c_prompt, c_sample, jnp.zeros((-n_seq % 8, d), F32)], axis=0)
    params = dict(
        w_ada=w_ada[0], b_ada=b_ada[0], norm_mix=norm_mix[0], w_in=w_in[0],
        ssm_a_re=ssm_a_re[0], ssm_a_im=ssm_a_im[0], ssm_log_dt=ssm_log_dt[0],
        ssm_b_re=ssm_b_re[0], ssm_b_im=ssm_b_im[0], ssm_c_re=ssm_c_re[0], ssm_c_im=ssm_c_im[0],
        ssm_d=ssm_d[0], w_ssm_glu=w_ssm_glu[0], w_branch_ssm=w_branch_ssm[0],
        conv_w=conv_w[0], conv_b=conv_b[0], conv_norm_g=conv_norm_g[0], conv_norm_b=conv_norm_b[0],
        w_branch_conv=w_branch_conv[0], w_merge_gate=w_merge_gate[0], b_merge_gate=b_merge_gate[0],
        w_out=w_out[0], norm_ffn=norm_ffn[0], router_w=router_w[0], router_bias=router_bias[0],
        exp_w_gate=exp_w_gate[0], exp_w_up=exp_w_up[0], exp_w_down=exp_w_down[0],
        shared_w_gate=shared_w_gate[0], shared_w_up=shared_w_up[0], shared_w_down=shared_w_down[0],
        final_norm=final_norm)
    y_a, y_b = _layer(x_prompt.reshape(bp * lp, d), x_sample.reshape(bs * ls, d), c_all, seqs, params)
    return (y_a.reshape(bp, lp, d), y_b.reshape(bs, ls, d))
```

```python
import functools
import math

import jax
import jax.numpy as jnp
import numpy as np
from jax import lax
from jax.experimental import pallas as pl
from jax.experimental.pallas import tpu as pltpu

F32 = jnp.float32
BF16 = jnp.bfloat16
I32 = jnp.int32
U32 = jnp.uint32

SSM_GROUP_CH = 16
SSM_CHUNK = 16
N_ROUTE_GROUPS = 8
TOPK_ROUTE_GROUPS = 4
TOP_K = 8
ROUTED_SCALE = 2.5
RMS_EPS = 1e-6
LN_EPS = 1e-5

V7X_VMEM_LIMIT = 56 * 1024 * 1024
LANES = 128


def _cparams(sem, vmem=V7X_VMEM_LIMIT):
    return pltpu.CompilerParams(dimension_semantics=sem, vmem_limit_bytes=vmem)


def _tile(n, want):
    t = min(n, want)
    while n % t:
        t -= 1
    return t


def _sigmoid(x):
    return 1.0 / (1.0 + jnp.exp(-x))


def _bdot(a, b):
    return jnp.dot(a, b, preferred_element_type=F32)


def _ada_kernel(c_ref, w_ref, b_ref, o_ref):
    c = c_ref[...]
    s = (c * _sigmoid(c)).astype(BF16)
    o_ref[...] = _bdot(s, w_ref[...].astype(BF16)) + b_ref[...]


def _ada(c_all, w_ada, b_ada):
    nseq, d = c_all.shape
    n = w_ada.shape[1]
    tn = _tile(n, 512)
    return pl.pallas_call(
        _ada_kernel,
        out_shape=jax.ShapeDtypeStruct((nseq, n), F32),
        grid=(n // tn,),
        in_specs=[pl.BlockSpec((nseq, d), lambda j: (0, 0)),
                  pl.BlockSpec((d, tn), lambda j: (0, j)),
                  pl.BlockSpec((1, tn), lambda j: (0, j))],
        out_specs=pl.BlockSpec((nseq, tn), lambda j: (0, j)),
        compiler_params=_cparams(("arbitrary",)),
        name="ada",
    )(c_all, w_ada, b_ada.reshape(1, n))


def _pack_bf16_pair(lo, hi):
    lo_b = pltpu.bitcast(lo.astype(BF16).astype(F32), U32) >> 16
    hi_b = pltpu.bitcast(hi.astype(BF16).astype(F32), U32) & jnp.uint32(0xFFFF0000)
    return hi_b | lo_b


def _unpack_bf16_pair(w):
    lo = pltpu.bitcast(w << 16, F32)
    hi = pltpu.bitcast(w & jnp.uint32(0xFFFF0000), F32)
    return lo, hi


def _store_row_slabs(ref, val, pitch):
    rows, width = val.shape
    for j in range(width // LANES):
        ref[pl.ds(j, rows, stride=pitch), :] = val[:, j * LANES:(j + 1) * LANES]


def _load_row_slabs(ref, rows, n_tiles, pitch):
    return jnp.concatenate([ref[pl.ds(j, rows, stride=pitch), :] for j in range(n_tiles)], axis=1)


def _modulated_norm(x, g, sc, sh):
    ms = jnp.mean(x * x, axis=-1, keepdims=True)
    return x * lax.rsqrt(ms + RMS_EPS) * g * (1.0 + sc) + sh


def _two_source_specs(block, n_a, n_inner=0):
    if n_inner:
        first = lambda i, j, *_: (jnp.minimum(i, n_a - 1), jnp.where(i < n_a, j, 0))
        second = lambda i, j, *_: (jnp.maximum(i - n_a, 0), jnp.where(i < n_a, 0, j))
    else:
        first = lambda i, *_: (jnp.minimum(i, n_a - 1), 0)
        second = lambda i, *_: (jnp.maximum(i - n_a, 0), 0)
    return pl.BlockSpec(block, first), pl.BlockSpec(block, second)


def _pick_source(i, n_a, a_ref, b_ref):
    return jnp.where(i < n_a, a_ref[...], b_ref[...])


def _norm1_kernel(seq_ref, xa_ref, xb_ref, g_ref, sc_ref, sh_ref, o_ref, *, n_a):
    del seq_ref
    x = _pick_source(pl.program_id(0), n_a, xa_ref, xb_ref)
    o_ref[...] = _modulated_norm(x, g_ref[...], sc_ref[0], sh_ref[0]).astype(BF16)


def _norm2_kernel(seq_ref, x_ref, g_ref, sc_ref, sh_ref, rw_ref, h_ref, hp_ref, lg_ref):
    del seq_ref
    h = _modulated_norm(x_ref[...], g_ref[...], sc_ref[0], sh_ref[0])
    half = h.shape[1] // 2
    h_ref[...] = h.astype(BF16)
    _store_row_slabs(hp_ref, _pack_bf16_pair(h[:, :half], h[:, half:]), half // LANES)
    lg_ref[...] = jnp.dot(h, rw_ref[...], preferred_element_type=F32,
                          precision=lax.Precision.HIGHEST)


def _norm1(xa, xb, g, sc, sh, seq_of_tile, tm):
    d = xa.shape[1]
    t = xa.shape[0] + xb.shape[0]
    n_a = xa.shape[0] // tm
    row = lambda i, s: (i, 0)
    per_seq = lambda i, s: (s[i], 0, 0)
    return pl.pallas_call(
        functools.partial(_norm1_kernel, n_a=n_a),
        out_shape=jax.ShapeDtypeStruct((t, d), BF16),
        grid_spec=pltpu.PrefetchScalarGridSpec(
            num_scalar_prefetch=1, grid=(t // tm,),
            in_specs=[*_two_source_specs((tm, d), n_a),
                      pl.BlockSpec((1, d), lambda i, s: (0, 0)),
                      pl.BlockSpec((1, 1, d), per_seq),
                      pl.BlockSpec((1, 1, d), per_seq)],
            out_specs=pl.BlockSpec((tm, d), row)),
        compiler_params=_cparams(("arbitrary",)),
        name="norm1",
    )(seq_of_tile, xa, xb, g.reshape(1, d), sc, sh)


def _norm2(x, g, sc, sh, router_w, seq_of_tile, tm):
    t, d = x.shape
    e = router_w.shape[1]
    slab_rows = d // 2 // LANES
    row = lambda i, s: (i, 0)
    per_seq = lambda i, s: (s[i], 0, 0)
    return pl.pallas_call(
        _norm2_kernel,
        out_shape=(jax.ShapeDtypeStruct((t, d), BF16),
                   jax.ShapeDtypeStruct((t * slab_rows, LANES), U32),
                   jax.ShapeDtypeStruct((t, e), F32)),
        grid_spec=pltpu.PrefetchScalarGridSpec(
            num_scalar_prefetch=1, grid=(t // tm,),
            in_specs=[pl.BlockSpec((tm, d), row),
                      pl.BlockSpec((1, d), lambda i, s: (0, 0)),
                      pl.BlockSpec((1, 1, d), per_seq),
                      pl.BlockSpec((1, 1, d), per_seq),
                      pl.BlockSpec((d, e), lambda i, s: (0, 0))],
            out_specs=(pl.BlockSpec((tm, d), row),
                       pl.BlockSpec((tm * slab_rows, LANES), row),
                       pl.BlockSpec((tm, e), row))),
        compiler_params=_cparams(("arbitrary",)),
        name="norm2_router",
    )(seq_of_tile, x, g.reshape(1, d), sc, sh, router_w)


def _proj_kernel(x_ref, w_ref, o_ref):
    o_ref[...] = _bdot(x_ref[...], w_ref[...]).astype(o_ref.dtype)


def _proj(x, w, tm, tn, out_dtype=BF16, name="proj"):
    t, k = x.shape
    n = w.shape[1]
    return pl.pallas_call(
        _proj_kernel,
        out_shape=jax.ShapeDtypeStruct((t, n), out_dtype),
        grid=(t // tm, n // tn),
        in_specs=[pl.BlockSpec((tm, k), lambda i, j: (i, 0)),
                  pl.BlockSpec((k, tn), lambda i, j: (0, j))],
        out_specs=pl.BlockSpec((tm, tn), lambda i, j: (i, j)),
        compiler_params=_cparams(("parallel", "arbitrary")),
        name=name,
    )(x, w)


def _glu_kernel(*refs, silu_first, n_a):
    if n_a is None:
        x_ref, wa_ref, wb_ref, o_ref = refs
        x = x_ref[...].astype(BF16)
    else:
        xa_ref, xb_ref, wa_ref, wb_ref, o_ref = refs
        x = _pick_source(pl.program_id(0), n_a, xa_ref, xb_ref).astype(BF16)
    a = _bdot(x, wa_ref[...])
    b = _bdot(x, wb_ref[...])
    if silu_first:
        y = a * _sigmoid(a) * b
    else:
        y = a * _sigmoid(b)
    o_ref[...] = y.astype(o_ref.dtype)


def _glu_proj(x, wa, wb, tm, tn, silu_first, name, x_tail=None):
    k = x.shape[1]
    n = wa.shape[1]
    if x_tail is None:
        t, n_a = x.shape[0], None
        x_args, x_specs = [x], [pl.BlockSpec((tm, k), lambda i, j: (i, 0))]
    else:
        t, n_a = x.shape[0] + x_tail.shape[0], x.shape[0] // tm
        x_args, x_specs = [x, x_tail], list(_two_source_specs((tm, k), n_a))
    return pl.pallas_call(
        functools.partial(_glu_kernel, silu_first=silu_first, n_a=n_a),
        out_shape=jax.ShapeDtypeStruct((t, n), BF16),
        grid=(t // tm, n // tn),
        in_specs=x_specs + [pl.BlockSpec((k, tn), lambda i, j: (0, j)),
                            pl.BlockSpec((k, tn), lambda i, j: (0, j))],
        out_specs=pl.BlockSpec((tm, tn), lambda i, j: (i, j)),
        compiler_params=_cparams(("parallel", "arbitrary")),
        name=name,
    )(*x_args, wa, wb)


def _merge_kernel(h_ref, a_ref, c_ref, wga_ref, wgb_ref, wa_ref, wc_ref, ba_ref, bb_ref, o_ref):
    h = h_ref[...]
    ga = _sigmoid(_bdot(h, wga_ref[...]) + ba_ref[...])
    gb = _sigmoid(_bdot(h, wgb_ref[...]) + bb_ref[...])
    ya = _bdot(a_ref[...], wa_ref[...])
    yb = _bdot(c_ref[...], wc_ref[...])
    o_ref[...] = (ga * ya + gb * yb).astype(o_ref.dtype)


def _merge(h, a, cv, wga, wgb, wa, wc, ba, bb, tm, tn):
    t, d = h.shape
    ka = a.shape[1]
    kc = cv.shape[1]
    row = lambda i, j: (i, 0)
    col = lambda i, j: (0, j)
    return pl.pallas_call(
        _merge_kernel,
        out_shape=jax.ShapeDtypeStruct((t, d), BF16),
        grid=(t // tm, d // tn),
        in_specs=[pl.BlockSpec((tm, d), row), pl.BlockSpec((tm, ka), row), pl.BlockSpec((tm, kc), row),
                  pl.BlockSpec((d, tn), col), pl.BlockSpec((d, tn), col),
                  pl.BlockSpec((ka, tn), col), pl.BlockSpec((kc, tn), col),
                  pl.BlockSpec((1, tn), col), pl.BlockSpec((1, tn), col)],
        out_specs=pl.BlockSpec((tm, tn), lambda i, j: (i, j)),
        compiler_params=_cparams(("parallel", "arbitrary")),
        name="merge",
    )(h, a, cv, wga, wgb, wa, wc, ba, bb)


def _outproj_kernel(seq_ref, m_ref, w_ref, xa_ref, xb_ref, g_ref, o_ref, *, n_a):
    del seq_ref
    x = _pick_source(pl.program_id(0), n_a, xa_ref, xb_ref)
    o_ref[...] = x + g_ref[0] * _bdot(m_ref[...], w_ref[...])


def _outproj(m, w, xa, xb, gate, seq_of_tile, tm, tn):
    t, k = m.shape
    d = w.shape[1]
    n_a = xa.shape[0] // tm
    return pl.pallas_call(
        functools.partial(_outproj_kernel, n_a=n_a),
        out_shape=jax.ShapeDtypeStruct((t, d), F32),
        grid_spec=pltpu.PrefetchScalarGridSpec(
            num_scalar_prefetch=1, grid=(t // tm, d // tn),
            in_specs=[pl.BlockSpec((tm, k), lambda i, j, s: (i, 0)),
                      pl.BlockSpec((k, tn), lambda i, j, s: (0, j)),
                      *_two_source_specs((tm, tn), n_a, n_inner=1),
                      pl.BlockSpec((1, 1, tn), lambda i, j, s: (s[i], 0, j))],
            out_specs=pl.BlockSpec((tm, tn), lambda i, j, s: (i, j))),
        compiler_params=_cparams(("parallel", "arbitrary")),
        name="outproj",
    )(seq_of_tile, m, w, xa, xb, gate)


def _lagmat_kernel(c_ref, b_ref, o_ref):
    for g in range(c_ref.shape[0]):
        o_ref[g] = jnp.dot(c_ref[g], b_ref[g], preferred_element_type=F32,
                           precision=lax.Precision.HIGHEST)


def _lagmat(cs, bs):
    g2, c, n2 = cs.shape
    w = bs.shape[2]
    gb = _tile(g2, 8)
    return pl.pallas_call(
        _lagmat_kernel,
        out_shape=jax.ShapeDtypeStruct((g2, c, w), F32),
        grid=(g2 // gb,),
        in_specs=[pl.BlockSpec((gb, c, n2), lambda i: (i, 0, 0)),
                  pl.BlockSpec((gb, n2, w), lambda i: (i, 0, 0))],
        out_specs=pl.BlockSpec((gb, c, w), lambda i: (i, 0, 0)),
        compiler_params=_cparams(("arbitrary",)),
        name="ssm_lagmat",
    )(cs, bs)


def _ssm_tables(a_re, a_im, log_dt, b_re, b_im, c_re, c_im, d_skip):
    tc, ch = SSM_CHUNK, SSM_GROUP_CH
    _, g, n = a_re.shape
    dt = jnp.exp(log_dt)[:, :, None]
    zr, zi = a_re * dt, a_im * dt
    ks = jnp.arange(tc + 1, dtype=F32)[:, None, None, None]
    mag = jnp.exp(ks * zr[None])
    pw_re, pw_im = mag * jnp.cos(ks * zi[None]), mag * jnp.sin(ks * zi[None])
    lb_re, lb_im = pw_re[1], pw_im[1]
    den = a_re * a_re + a_im * a_im
    q_re = ((lb_re - 1.0) * a_re + lb_im * a_im) / den
    q_im = (lb_im * a_re - (lb_re - 1.0) * a_im) / den
    bb_re = q_re[..., None] * b_re - q_im[..., None] * b_im
    bb_im = q_re[..., None] * b_im + q_im[..., None] * b_re
    p_re, p_im = pw_re[:tc, ..., None], pw_im[:tc, ..., None]
    bk_re = p_re * bb_re[None] - p_im * bb_im[None]
    bk_im = p_re * bb_im[None] + p_im * bb_re[None]

    bs = jnp.concatenate([bk_re, bk_im], axis=3)
    bs = bs.transpose(1, 2, 3, 0, 4).reshape(2 * g, 2 * n, tc * ch)
    cs = jnp.concatenate([c_re, -c_im], axis=2)
    cs = jnp.broadcast_to(cs[None], (2, g, ch, 2 * n)).reshape(2 * g, ch, 2 * n)
    m = _lagmat(cs, bs).reshape(2, g, ch, tc, ch)

    s_idx = np.arange(tc)[:, None]
    t_idx = np.arange(tc)[None, :]
    lag_f = np.clip(t_idx - s_idx, 0, tc - 1)
    lag_b = np.clip(s_idx - t_idx, 0, tc - 1)
    mf = m[0][:, :, lag_f, :] * jnp.asarray(t_idx >= s_idx, F32)[None, None, :, :, None]
    mb = m[1][:, :, lag_b, :] * jnp.asarray(s_idx >= t_idx, F32)[None, None, :, :, None]
    toep = (mf + mb).transpose(0, 2, 4, 3, 1).reshape(g, tc * ch, tc * ch)

    def rows_sc(x):
        return x.transpose(1, 0, 3, 2).reshape(g, tc * ch, n)
    pin = jnp.concatenate([rows_sc(bk_re[::-1, 0]), rows_sc(bk_re[:, 1]),
                           rows_sc(bk_im[::-1, 0]), rows_sc(bk_im[:, 1])], axis=2)

    def w_tab(pr, pi):
        wr = c_re[None] * pr[:, :, None, :] - c_im[None] * pi[:, :, None, :]
        wi = c_re[None] * pi[:, :, None, :] + c_im[None] * pr[:, :, None, :]
        to_rows = lambda x: x.transpose(1, 3, 0, 2).reshape(g, n, tc * ch)
        return to_rows(wr), to_rows(-wi)
    qf_re, qf_im = w_tab(pw_re[1:tc + 1, 0], pw_im[1:tc + 1, 0])
    qb_re, qb_im = w_tab(pw_re[tc:0:-1, 1], pw_im[tc:0:-1, 1])
    qout = jnp.concatenate([qf_re, qb_re, qf_im, qb_im], axis=1)

    a_step_re = jnp.concatenate([pw_re[tc, 0], pw_re[tc, 1]], axis=1)
    a_step_im = jnp.concatenate([pw_im[tc, 0], pw_im[tc, 1]], axis=1)
    d_tile = jnp.tile(d_skip.reshape(g, 1, ch), (1, tc, 1)).reshape(g, tc * ch)
    return (toep.astype(BF16), pin.astype(BF16), qout.astype(BF16),
            a_step_re, a_step_im, d_tile)


def _gelu_tanh(x):
    return 0.5 * x * (1.0 + jnp.tanh(math.sqrt(2.0 / math.pi) * (x + 0.044715 * (x * x * x))))


def _ssm_kernel(u_ref, t_ref, p_ref, q_ref, are_ref, aim_ref, d_ref, o_ref,
                ug_ref, yg_ref, sin_re, sin_im, hf_re, hf_im, hb_re, hb_im):
    gb, w, _ = t_ref.shape
    tc, ch = SSM_CHUNK, SSM_GROUP_CH
    nc = u_ref.shape[0] // tc
    half = w // 2
    n_col = w // LANES
    slots = LANES // ch
    slot = lax.broadcasted_iota(I32, (8, LANES), 1) // ch
    in_slot = [slot == k for k in range(slots)]

    def to_chunks(rc, _):
        xs = [u_ref[pl.ds(rc * (8 * tc) + t, 8, stride=tc), :] for t in range(tc)]
        cols = [[jnp.zeros((8, LANES), F32) for _ in range(n_col)] for _ in range(gb)]
        for col in range(n_col):
            for s in range(slots):
                v = jnp.zeros((8, LANES), F32)
                for ts in range(slots):
                    v = jnp.where(in_slot[(ts - s) % slots], xs[col * slots + ts], v)
                vr = v if s == 0 else pltpu.roll(v, s * ch, 1)
                for ts in range(slots):
                    g = (ts - s) % slots
                    cols[g][col] = jnp.where(in_slot[ts], vr, cols[g][col])
        for g in range(gb):
            ug_ref[g, pl.ds(rc * 8, 8), :] = jnp.concatenate(cols[g], axis=1)
        return 0

    lax.fori_loop(0, nc // 8, to_chunks, 0)

    for g in range(gb):
        s_in = _bdot(ug_ref[g].astype(BF16), p_ref[g])
        sin_re[pl.ds(g, nc, stride=gb), :] = s_in[:, :half]
        sin_im[pl.ds(g, nc, stride=gb), :] = s_in[:, half:]

    a_re = are_ref[...]
    a_im = aim_ref[...]
    is_fwd = lax.broadcasted_iota(I32, (gb, half), 1) < (half // 2)

    def step(j, carry):
        s_re, s_im = carry
        at_f = pl.ds(pl.multiple_of(j * gb, gb), gb)
        at_b = pl.ds(pl.multiple_of((nc - 1 - j) * gb, gb), gb)
        hf_re[at_f, :] = s_re
        hf_im[at_f, :] = s_im
        hb_re[at_b, :] = s_re
        hb_im[at_b, :] = s_im
        in_re = jnp.where(is_fwd, sin_re[at_f, :], sin_re[at_b, :])
        in_im = jnp.where(is_fwd, sin_im[at_f, :], sin_im[at_b, :])
        return (a_re * s_re - a_im * s_im + in_re, a_re * s_im + a_im * s_re + in_im)

    zero = jnp.zeros((gb, half), F32)
    lax.fori_loop(0, nc, step, (zero, zero))

    fwd_lane = lax.broadcasted_iota(I32, (nc, half), 1) < (half // 2)
    for g in range(gb):
        u = ug_ref[g]
        rows = pl.ds(g, nc, stride=gb)
        hin = jnp.concatenate([jnp.where(fwd_lane, hf_re[rows, :], hb_re[rows, :]),
                               jnp.where(fwd_lane, hf_im[rows, :], hb_im[rows, :])], axis=1)
        y = _bdot(u.astype(BF16), t_ref[g]) + _bdot(hin.astype(BF16), q_ref[g]) + d_ref[pl.ds(g, 1), :] * u
        yg_ref[g] = _gelu_tanh(y)

    def from_chunks(rc, _):
        for col in range(n_col):
            ys = [yg_ref[g, pl.ds(rc * 8, 8), pl.ds(col * LANES, LANES)] for g in range(gb)]
            z = [jnp.zeros((8, LANES), F32) for _ in range(slots)]
            for s in range(slots):
                v = jnp.zeros((8, LANES), F32)
                for g in range(gb):
                    v = jnp.where(in_slot[(g - s) % slots], ys[g], v)
                vr = v if s == 0 else pltpu.roll(v, s * ch, 1)
                for g in range(gb):
                    ts = (g - s) % slots
                    z[ts] = jnp.where(in_slot[g], vr, z[ts])
            for ts in range(slots):
                o_ref[pl.ds(rc * (8 * tc) + col * slots + ts, 8, stride=tc), :] = z[ts]
        return 0

    lax.fori_loop(0, nc // 8, from_chunks, 0)


def _ssm_seqs(u, tabs, row0, n_seq, seq_len):
    toep, pin, qout, a_re, a_im, d_tile = tabs
    _, wch = u.shape
    w = toep.shape[1]
    half = w // 2
    gb = LANES // SSM_GROUP_CH
    nc = seq_len // SSM_CHUNK
    assert nc % 8 == 0 and row0 % seq_len == 0 and wch % LANES == 0
    blk0 = row0 // seq_len
    tab_spec = pl.BlockSpec((gb, w, w), lambda i, b: (i, 0, 0))
    vec_spec = lambda width: pl.BlockSpec((gb, width), lambda i, b: (i, 0))
    return pl.pallas_call(
        _ssm_kernel,
        out_shape=jax.ShapeDtypeStruct((n_seq * seq_len, wch), F32),
        grid=(wch // LANES, n_seq),
        in_specs=[pl.BlockSpec((seq_len, LANES), lambda i, b: (blk0 + b, i)),
                  tab_spec, tab_spec, tab_spec,
                  vec_spec(half), vec_spec(half), vec_spec(w)],
        out_specs=pl.BlockSpec((seq_len, LANES), lambda i, b: (b, i)),
        scratch_shapes=[pltpu.VMEM((gb, nc, w), F32)] * 2 + [pltpu.VMEM((gb * nc, half), F32)] * 6,
        compiler_params=_cparams(("parallel", "arbitrary")),
        name="ssm",
    )(u, toep, pin, qout, a_re, a_im, d_tile)


def _ssm_branch(u, tabs, seqs):
    outs = []
    row0 = 0
    for n_seq, seq_len in seqs:
        outs.append(_ssm_seqs(u, tabs, row0, n_seq, seq_len))
        row0 += n_seq * seq_len
    return outs


def _conv_kernel(first_ref, last_ref, prev_ref, cur_ref, next_ref, w_ref, b_ref, g_ref, beta_ref,
                 o_ref, xs_ref, acc_ref, *, halo, row_chunk):
    i = pl.program_id(0)
    r, c = cur_ref.shape
    kw = w_ref.shape[0]
    half = kw // 2
    n_shift = xs_ref.shape[0]
    keep_prev = jnp.where(first_ref[i] == 0, 1.0, 0.0)
    keep_next = jnp.where(last_ref[i] == 0, 1.0, 0.0)
    xs_ref[0, pl.ds(0, halo), :] = prev_ref[...].astype(F32) * keep_prev
    xs_ref[0, pl.ds(halo, r), :] = cur_ref[...].astype(F32)
    xs_ref[0, pl.ds(halo + r, halo), :] = next_ref[...].astype(F32) * keep_next
    span = r + 2 * halo - n_shift
    for b in range(1, n_shift):
        xs_ref[b, pl.ds(0, span), :] = xs_ref[0, pl.ds(b, span), :]

    col_chunk = _tile(c, 4 * LANES)
    n_cc = c // col_chunk

    def chunk(it, _):
        r0 = pl.multiple_of((it // n_cc) * row_chunk, row_chunk)
        c0 = pl.multiple_of((it % n_cc) * col_chunk, col_chunk)
        cols = pl.ds(c0, col_chunk)
        acc = jnp.zeros((row_chunk, col_chunk), F32)
        for k in range(kw):
            off = halo - half + k
            b = off % n_shift
            acc = acc + w_ref[pl.ds(k, 1), cols] * xs_ref[b, pl.ds(r0 + (off - b), row_chunk), cols]
        acc_ref[pl.ds(r0, row_chunk), cols] = acc
        return 0

    lax.fori_loop(0, (r // row_chunk) * n_cc, chunk, 0)
    ln_rows = _tile(r, 16)

    def norm_rows(it, _):
        rows = pl.ds(pl.multiple_of(it * ln_rows, ln_rows), ln_rows)
        y = acc_ref[rows, :] + b_ref[...]
        mu = jnp.mean(y, axis=-1, keepdims=True)
        yc = y - mu
        var = jnp.mean(yc * yc, axis=-1, keepdims=True)
        z = yc * lax.rsqrt(var + LN_EPS) * g_ref[...] + beta_ref[...]
        o_ref[rows, :] = (z * _sigmoid(z)).astype(o_ref.dtype)
        return 0

    lax.fori_loop(0, r // ln_rows, norm_rows, 0, unroll=2)


def _conv_ln(ub, conv_w, conv_b, ln_g, ln_b, first_of_tile, last_of_tile, r):
    t, c = ub.shape
    kw = conv_w.shape[0]
    halo = 16
    assert kw // 2 <= halo and r % halo == 0
    nb = r // halo
    n_halo_blocks = t // halo
    row_chunk = _tile(r, 32)
    vec = lambda i, f, l: (0, 0)
    return pl.pallas_call(
        functools.partial(_conv_kernel, halo=halo, row_chunk=row_chunk),
        out_shape=jax.ShapeDtypeStruct((t, c), BF16),
        grid_spec=pltpu.PrefetchScalarGridSpec(
            num_scalar_prefetch=2, grid=(t // r,),
            in_specs=[pl.BlockSpec((halo, c), lambda i, f, l: (jnp.maximum(i * nb - 1, 0), 0)),
                      pl.BlockSpec((r, c), lambda i, f, l: (i, 0)),
                      pl.BlockSpec((halo, c), lambda i, f, l: (jnp.minimum((i + 1) * nb, n_halo_blocks - 1), 0)),
                      pl.BlockSpec((kw, c), vec), pl.BlockSpec((1, c), vec),
                      pl.BlockSpec((1, c), vec), pl.BlockSpec((1, c), vec)],
            out_specs=pl.BlockSpec((r, c), lambda i, f, l: (i, 0)),
            scratch_shapes=[pltpu.VMEM((8, r + 2 * halo, c), F32), pltpu.VMEM((r, c), F32)]),
        compiler_params=_cparams(("arbitrary",)),
        name="conv_ln",
    )(first_of_tile, last_of_tile, ub, ub, ub, conv_w, conv_b.reshape(1, c),
      ln_g.reshape(1, c), ln_b.reshape(1, c))


def _seg_allreduce(x, lane, seg, op):
    n = x.shape[-1]
    s = 1
    while s < seg:
        up = pltpu.roll(x, n - s, 1)
        dn = pltpu.roll(x, s, 1)
        x = op(x, jnp.where((lane & s) == 0, up, dn))
        s *= 2
    return x


def _route_kernel(lg_ref, bias_ref, idx_ref, wt_ref, rank_ref, cnt_ref, carry_ref, *, n_groups, topk_groups, top_k):
    i = pl.program_id(0)
    tm, e = lg_ref.shape
    seg = e // n_groups
    lane = lax.broadcasted_iota(I32, (tm, e), 1)
    lane_f = lane.astype(F32)
    neg = jnp.float32(-jnp.inf)

    scores = _sigmoid(lg_ref[...])
    choice = scores + bias_ref[...]

    m1 = _seg_allreduce(choice, lane, seg, jnp.maximum)
    first = _seg_allreduce(jnp.where(choice == m1, lane_f, float(e)), lane, seg, jnp.minimum)
    m2 = _seg_allreduce(jnp.where(lane_f == first, neg, choice), lane, seg, jnp.maximum)
    gscore = m1 + m2

    beaten = jnp.zeros((tm, e), F32)
    for dshift in range(1, n_groups):
        other = pltpu.roll(gscore, dshift * seg, 1)
        other_is_lower = lane >= dshift * seg
        beaten = beaten + jnp.where(other_is_lower, jnp.where(other >= gscore, 1.0, 0.0),
                                    jnp.where(other > gscore, 1.0, 0.0))
    masked = jnp.where(beaten < topk_groups, choice, neg)

    idx_out = jnp.zeros((tm, e), F32)
    sc_out = jnp.zeros((tm, e), F32)
    sel = jnp.zeros((tm, e), F32)
    picks = []
    for k in range(top_k):
        m = jnp.max(masked, axis=1, keepdims=True)
        pick = jnp.min(jnp.where(masked == m, lane_f, float(e)), axis=1, keepdims=True)
        hit = lane_f == pick
        s_k = jnp.sum(jnp.where(hit, scores, 0.0), axis=1, keepdims=True)
        idx_out = jnp.where(lane == k, pick, idx_out)
        sc_out = jnp.where(lane == k, s_k, sc_out)
        sel = jnp.where(hit, 1.0, sel)
        masked = jnp.where(hit, neg, masked)
        picks.append(hit)
    denom = jnp.sum(sc_out, axis=1, keepdims=True)
    idx_ref[...] = idx_out.astype(I32)
    wt_ref[...] = sc_out / denom * ROUTED_SCALE

    @pl.when(i == 0)
    def _():
        carry_ref[...] = jnp.zeros_like(carry_ref)

    rows = lax.broadcasted_iota(I32, (tm, tm), 0)
    cols = lax.broadcasted_iota(I32, (tm, tm), 1)
    tri = jnp.where(cols < rows, 1.0, 0.0).astype(BF16)
    rank = _bdot(tri, sel.astype(BF16)) + carry_ref[...]
    rank_out = jnp.zeros((tm, e), F32)
    for k in range(top_k):
        r_k = jnp.sum(jnp.where(picks[k], rank, 0.0), axis=1, keepdims=True)
        rank_out = jnp.where(lane == k, r_k, rank_out)
    rank_ref[...] = rank_out.astype(I32)
    carry_ref[...] = carry_ref[...] + jnp.sum(sel, axis=0, keepdims=True)
    cnt_ref[...] = carry_ref[...].astype(I32)


def _route(logits, bias, tm):
    t, e = logits.shape
    row = lambda i: (i, 0)
    fixed = lambda i: (0, 0)
    return pl.pallas_call(
        functools.partial(_route_kernel, n_groups=N_ROUTE_GROUPS, topk_groups=TOPK_ROUTE_GROUPS, top_k=TOP_K),
        out_shape=(jax.ShapeDtypeStruct((t, e), I32), jax.ShapeDtypeStruct((t, e), F32),
                   jax.ShapeDtypeStruct((t, e), I32), jax.ShapeDtypeStruct((1, e), I32)),
        grid=(t // tm,),
        in_specs=[pl.BlockSpec((tm, e), row), pl.BlockSpec((1, e), fixed)],
        out_specs=(pl.BlockSpec((tm, e), row), pl.BlockSpec((tm, e), row),
                   pl.BlockSpec((tm, e), row), pl.BlockSpec((1, e), fixed)),
        scratch_shapes=[pltpu.VMEM((1, e), F32)],
        compiler_params=_cparams(("arbitrary",)),
        name="route",
    )(logits, bias.reshape(1, e))


def _dest_kernel(idx_ref, rank_ref, start_ref, o_ref, *, top_k):
    tm, e = idx_ref.shape
    lane = lax.broadcasted_iota(I32, (tm, e), 1)
    idx = idx_ref[...]
    start = start_ref[...].astype(F32)
    out = jnp.zeros((tm, e), F32)
    for k in range(top_k):
        s_k = jnp.sum(jnp.where(lane == idx[:, k:k + 1], start, 0.0), axis=1, keepdims=True)
        out = jnp.where(lane == k, s_k, out)
    o_ref[...] = out.astype(I32) + rank_ref[...]


def _dest_rows(top_idx, rank, pad_start, tm):
    t, e = top_idx.shape
    row = lambda i: (i, 0)
    return pl.pallas_call(
        functools.partial(_dest_kernel, top_k=TOP_K),
        out_shape=jax.ShapeDtypeStruct((t, e), I32),
        grid=(t // tm,),
        in_specs=[pl.BlockSpec((tm, e), row), pl.BlockSpec((tm, e), row), pl.BlockSpec((1, e), lambda i: (0, 0))],
        out_specs=pl.BlockSpec((tm, e), row),
        compiler_params=_cparams(("arbitrary",)),
        name="dest_rows",
    )(top_idx, rank, pad_start.reshape(1, e))


def _expert_kernel(blk_e_ref, nblk_ref, idx_hbm, x_hbm, wg_ref, wu_ref, wd_ref, ys_hbm,
                   src_smem, dst_smem, xb0, xb1, ob0, ob1, gate_ref, up_ref, act_ref,
                   sem_src, sem_dst, sem_g, sem_s):
    del blk_e_ref
    b = pl.program_id(0)
    n = nblk_ref[0]
    kh = wg_ref.shape[1] // 2
    ns = kh // LANES
    pitch = ns + 1
    r = xb0.shape[0] // pitch
    xbufs = (xb0, xb1)
    obufs = (ob0, ob1)
    dump0 = ys_hbm.shape[0] // ns - 2 * r
    dump_blk = idx_hbm.shape[0] - 1

    def src_copy(blk, s):
        return pltpu.make_async_copy(idx_hbm.at[blk, pl.ds(0, 1)], src_smem.at[s], sem_src.at[s])

    def dst_copy(blk, s):
        return pltpu.make_async_copy(idx_hbm.at[blk, pl.ds(1, 1)], dst_smem.at[s], sem_dst.at[s])

    def hbm_row(ref, row):
        return ref.at[pl.ds(pl.multiple_of(row * ns, ns), ns)]

    def vmem_row(ref, i):
        return ref.at[pl.ds(i * pitch, ns)]

    def gather_start(s, rows):
        for i in rows:
            pltpu.make_async_copy(hbm_row(x_hbm, src_smem[s, 0, i]), vmem_row(xbufs[s], i), sem_g.at[s]).start()

    def gather_wait(s):
        for i in range(r):
            pltpu.make_async_copy(hbm_row(x_hbm, 0), vmem_row(xbufs[s], i), sem_g.at[s]).wait()

    def scatter_start(s, rows):
        for i in rows:
            pltpu.make_async_copy(vmem_row(obufs[s], i), hbm_row(ys_hbm, dst_smem[s, 0, i]), sem_s.at[s]).start()

    def scatter_wait(s):
        for i in range(r):
            pltpu.make_async_copy(vmem_row(obufs[s], i), hbm_row(ys_hbm, 0), sem_s.at[s]).wait()

    def start_and_wait(copy):
        copy.start()
        copy.wait()

    @pl.when(b == 0)
    def _():
        ob1[...] = jnp.zeros_like(ob1)
        for h in range(2):
            start_and_wait(pltpu.make_async_copy(ob1.at[pl.ds(0, r * ns)],
                                                 ys_hbm.at[pl.ds((dump0 + h * r) * ns, r * ns)], sem_s.at[1]))
        for s in range(2):
            start_and_wait(src_copy(0, s))
        gather_start(0, range(r))
        start_and_wait(dst_copy(dump_blk, 1))
        dst_copy(0, 0).start()

        @pl.when(n > 1)
        def _():
            src_copy(1, 1).start()

    def step(cur):
        nxt = 1 - cur
        gather_wait(cur)

        @pl.when(b + 1 < n)
        def _():
            src_copy(b + 1, nxt).wait()

        @pl.when(b >= 1)
        def _():
            dst_copy(b - 1, nxt).wait()
            scatter_wait(cur)

        n_long, n_short = 4, min(4, kh // LANES)
        unit = r // (2 * n_long + n_short)
        bounds = [0]
        for p in range(n_long + n_short):
            bounds.append(bounds[-1] + (2 * unit if p < n_long else unit))
        bounds[-1] = r

        def issue(p):
            @pl.when(n > 0)
            def _():
                rows = range(bounds[p], bounds[p + 1])
                gather_start(nxt, rows)
                scatter_start(nxt, rows)

        def x_half(which):
            words = _load_row_slabs(xbufs[cur], r, ns, pitch)
            return _unpack_bf16_pair(words)[which].astype(BF16)

        issue(0)
        gate_ref[...] = _bdot(x_half(0), wg_ref[0, pl.ds(0, kh), :])
        issue(1)
        gate_ref[...] += _bdot(x_half(1), wg_ref[0, pl.ds(kh, kh), :])
        issue(2)
        up_ref[...] = _bdot(x_half(0), wu_ref[0, pl.ds(0, kh), :])
        issue(3)
        gate = gate_ref[...]
        up = up_ref[...] + _bdot(x_half(1), wu_ref[0, pl.ds(kh, kh), :])
        act_ref[...] = (gate * _sigmoid(gate) * up).astype(BF16)
        wn = kh // n_short
        for c in range(n_short):
            issue(n_long + c)
            act = act_ref[...]
            out_lo = _bdot(act, wd_ref[0, :, pl.ds(c * wn, wn)])
            out_hi = _bdot(act, wd_ref[0, :, pl.ds(kh + c * wn, wn)])
            packed = _pack_bf16_pair(out_lo, out_hi)
            for j in range(wn // LANES):
                obufs[cur][pl.ds(c * (wn // LANES) + j, r, stride=pitch), :] = packed[:, j * LANES:(j + 1) * LANES]

        @pl.when(b + 2 < n)
        def _():
            src_copy(b + 2, cur).start()

        @pl.when(b + 1 < n)
        def _():
            dst_copy(b + 1, nxt).start()

        @pl.when(b == n - 1)
        def _():
            dst_copy(b, cur).wait()
            scatter_start(cur, range(r))
            gather_wait(nxt)
            scatter_wait(nxt)
            scatter_wait(cur)

    for parity in range(2):
        @pl.when((b < n) & ((b & 1) == parity))
        def _(parity=parity):
            step(parity)


def _experts(blk_e, nblk, row_idx, xp, wg, wu, wd, n_out_rows):
    n_blocks = row_idx.shape[0] - 1
    r = row_idx.shape[2]
    n_e, d, de = wg.shape
    ns = d // 2 // LANES
    last = lambda b, be, nb: jnp.minimum(b, nb[0] - 1)
    return pl.pallas_call(
        _expert_kernel,
        out_shape=jax.ShapeDtypeStruct(((n_out_rows + 2 * r) * ns, LANES), U32),
        grid_spec=pltpu.PrefetchScalarGridSpec(
            num_scalar_prefetch=2, grid=(n_blocks,),
            in_specs=[pl.BlockSpec(memory_space=pl.ANY),
                      pl.BlockSpec(memory_space=pl.ANY),
                      pl.BlockSpec((1, d, de), lambda b, be, nb: (be[last(b, be, nb)], 0, 0)),
                      pl.BlockSpec((1, d, de), lambda b, be, nb: (be[last(b, be, nb)], 0, 0)),
                      pl.BlockSpec((1, de, d), lambda b, be, nb: (be[last(b, be, nb)], 0, 0))],
            out_specs=pl.BlockSpec(memory_space=pl.ANY),
            scratch_shapes=[pltpu.SMEM((2, 1, r), I32)] * 2 + [pltpu.VMEM((r * (ns + 1), LANES), U32)] * 4
                           + [pltpu.VMEM((r, de), F32)] * 2 + [pltpu.VMEM((r, de), BF16)]
                           + [pltpu.SemaphoreType.DMA((2,))] * 4),
        compiler_params=_cparams(("arbitrary",)),
        name="experts",
    )(blk_e, nblk, row_idx, xp, wg, wu, wd)


def _combine_kernel(seq_ref, *refs, top_k, n_a):
    del seq_ref
    y_refs = refs[:top_k]
    wt_ref, act_ref, wsd_ref, x_ref, g2_ref, fn_ref, oa_ref, ob_ref, routed_ref = refs[top_k:]
    i = pl.program_id(0)
    tm, d = x_ref.shape
    kh = d // 2
    ns = kh // LANES
    wt = wt_ref[...]
    w_cols = [wt[:, k:k + 1] for k in range(top_k)]
    for j in range(ns):
        r_lo = jnp.zeros((tm, LANES), F32)
        r_hi = jnp.zeros((tm, LANES), F32)
        for k in range(top_k):
            lo, hi = _unpack_bf16_pair(y_refs[k][pl.ds(j, tm, stride=ns), :])
            r_lo = r_lo + w_cols[k] * lo
            r_hi = r_hi + w_cols[k] * hi
        routed_ref[:, pl.ds(j * LANES, LANES)] = r_lo
        routed_ref[:, pl.ds(kh + j * LANES, LANES)] = r_hi
    shared = _bdot(act_ref[...], wsd_ref[...])
    y = x_ref[...] + g2_ref[0] * (shared + routed_ref[...])
    ms = jnp.mean(y * y, axis=-1, keepdims=True)
    out = y * lax.rsqrt(ms + RMS_EPS) * fn_ref[...]

    @pl.when(i < n_a)
    def _():
        oa_ref[...] = out

    @pl.when(i >= n_a)
    def _():
        ob_ref[...] = out


def _combine(ys, wt, act, wsd, x1, g2, final_norm, seq_of_tile, tm, rows_a):
    t, d = x1.shape
    n_a = rows_a // tm
    e = wt.shape[1]
    ds_ = act.shape[1]
    ns = d // 2 // LANES
    nt = t // tm
    row = lambda i, s: (i, 0)
    fixed = lambda i, s: (0, 0)
    y_specs = [pl.BlockSpec((tm * ns, LANES), functools.partial(lambda i, s, k: (k * nt + i, 0), k=k))
               for k in range(TOP_K)]
    return pl.pallas_call(
        functools.partial(_combine_kernel, top_k=TOP_K, n_a=n_a),
        out_shape=(jax.ShapeDtypeStruct((rows_a, d), F32), jax.ShapeDtypeStruct((t - rows_a, d), F32)),
        grid_spec=pltpu.PrefetchScalarGridSpec(
            num_scalar_prefetch=1, grid=(nt,),
            in_specs=y_specs + [pl.BlockSpec((tm, e), row),
                                pl.BlockSpec((tm, ds_), row),
                                pl.BlockSpec((ds_, d), fixed),
                                pl.BlockSpec((tm, d), row),
                                pl.BlockSpec((1, 1, d), lambda i, s: (s[i], 0, 0)),
                                pl.BlockSpec((1, d), fixed)],
            out_specs=_two_source_specs((tm, d), n_a),
            scratch_shapes=[pltpu.VMEM((tm, d), F32)]),
        compiler_params=_cparams(("arbitrary",)),
        name="combine",
    )(seq_of_tile, *([ys] * TOP_K), wt, act, wsd, x1, g2, final_norm.reshape(1, d))


def _seq_tables(seqs, tile):
    sid, first, last = [], [], []
    s = 0
    for n_seq, seq_len in seqs:
        per = seq_len // tile
        for _ in range(n_seq):
            sid += [s] * per
            first += [1] + [0] * (per - 1)
            last += [0] * (per - 1) + [1]
            s += 1
    as_i32 = lambda v: jnp.asarray(np.asarray(v, np.int32))
    return as_i32(sid), as_i32(first), as_i32(last)


def _layer(xa, xb, c_all, seqs, p):
    d = xa.shape[1]
    t = xa.shape[0] + xb.shape[0]
    w_ssm = p["ssm_d"].shape[0]
    w_conv = p["conv_b"].shape[0]
    n_e = p["router_w"].shape[1]
    min_seq = min(sl for _, sl in seqs)

    tm = _tile(min_seq, 1024)
    tn_row = _tile(min_seq, 256)
    seq_mm, _, _ = _seq_tables(seqs, tm)
    seq_nr, _, _ = _seq_tables(seqs, tn_row)

    n_seq_total = c_all.shape[0]
    ada = _ada(c_all, p["w_ada"], p["b_ada"])
    sh1, sc1, g1, sh2, sc2, g2 = [a.reshape(n_seq_total, 1, d) for a in jnp.split(ada, 6, axis=1)]

    bf = lambda w: w.astype(BF16)
    w_in = p["w_in"]
    h = _norm1(xa, xb, p["norm_mix"], sc1, sh1, seq_nr, tn_row)

    u_ssm = _proj(h, bf(w_in[:, :w_ssm]), tm, _tile(w_ssm, 1024), out_dtype=F32, name="in_ssm")
    tabs = _ssm_tables(p["ssm_a_re"], p["ssm_a_im"], p["ssm_log_dt"], p["ssm_b_re"], p["ssm_b_im"],
                       p["ssm_c_re"], p["ssm_c_im"], p["ssm_d"])
    ya_a, ya_b = _ssm_branch(u_ssm, tabs, seqs)
    wgl = p["w_ssm_glu"]
    a_act = _glu_proj(ya_a, bf(wgl[:, :w_ssm]), bf(wgl[:, w_ssm:]), _tile(min_seq, 512), _tile(w_ssm, 1024), False,
                      "ssm_glu", x_tail=ya_b)

    ub = _glu_proj(h, bf(w_in[:, w_ssm:w_ssm + w_conv]), bf(w_in[:, w_ssm + w_conv:]),
                   tm, _tile(w_conv, 512), False, "in_conv_glu")
    r_conv = _tile(min_seq, 256)
    _, first_c, last_c = _seq_tables(seqs, r_conv)
    cv = _conv_ln(ub, p["conv_w"], p["conv_b"], p["conv_norm_g"], p["conv_norm_b"], first_c, last_c, r_conv)

    wmg = p["w_merge_gate"]
    bmg = p["b_merge_gate"]
    tm_merge = _tile(min_seq, 512)
    m = _merge(h, a_act, cv, bf(wmg[:, :d]), bf(wmg[:, d:]), bf(p["w_branch_ssm"]), bf(p["w_branch_conv"]),
               bmg[:d].reshape(1, d), bmg[d:].reshape(1, d), tm_merge, _tile(d, 512))
    x1 = _outproj(m, bf(p["w_out"]), xa, xb, g1, seq_mm, tm, _tile(d, 512))

    h2, h2p, logits = _norm2(x1, p["norm_ffn"], sc2, sh2, p["router_w"], seq_nr, tn_row)
    tm_route = _tile(t, 256)
    top_idx, top_w, rank, counts = _route(logits, p["router_bias"], tm_route)

    r_blk = 256 if t * TOP_K >= 256 * n_e else 8
    counts = counts[0]
    padded = (counts + r_blk - 1) // r_blk * r_blk
    pad_end = jnp.cumsum(padded)
    pad_start = pad_end - padded
    n_blocks = -(-(t * TOP_K) // r_blk) + n_e
    dest = _dest_rows(top_idx, rank, pad_start, tm_route)[:, :TOP_K]
    pos = jnp.arange(n_blocks * r_blk, dtype=I32)
    dump = TOP_K * t + ((pos // r_blk) % 2) * r_blk + pos % r_blk
    out_rows = jnp.arange(TOP_K, dtype=I32)[None, :] * t + jnp.arange(t, dtype=I32)[:, None]
    row_dst = dump.at[dest.reshape(-1)].set(out_rows.reshape(-1))
    row_src = jnp.where(row_dst < TOP_K * t, row_dst % t, 0)
    row_idx = jnp.stack([row_src.reshape(n_blocks, r_blk), row_dst.reshape(n_blocks, r_blk)], axis=1)
    dump_only = jnp.stack([jnp.zeros((r_blk,), I32), TOP_K * t + r_blk + jnp.arange(r_blk, dtype=I32)])
    row_idx = jnp.concatenate([row_idx, dump_only[None]], axis=0)
    blk_start = jnp.arange(n_blocks, dtype=I32) * r_blk
    blk_e = jnp.minimum(jnp.searchsorted(pad_end, blk_start, side="right"), n_e - 1).astype(I32)
    nblk = (pad_end[-1] // r_blk).astype(I32).reshape(1)

    ys = _experts(blk_e, nblk, row_idx, h2p, bf(p["exp_w_gate"]), bf(p["exp_w_up"]), bf(p["exp_w_down"]),
                  TOP_K * t)

    act = _glu_proj(h2, bf(p["shared_w_gate"]), bf(p["shared_w_up"]), tm, p["shared_w_gate"].shape[1],
                    True, "shared_act")
    tm_c = _tile(min_seq, 128)
    seq_c, _, _ = _seq_tables(seqs, tm_c)
    return _combine(ys, top_w, act, bf(p["shared_w_down"]), x1, g2, p["final_norm"], seq_c, tm_c, xa.shape[0])


def kernel(x_prompt, x_sample, c_prompt, c_sample, w_ada, b_ada, norm_mix, w_in, ssm_a_re, ssm_a_im, ssm_log_dt, ssm_b_re, ssm_b_im, ssm_c_re, ssm_c_im, ssm_d, w_ssm_glu, w_branch_ssm, conv_w, conv_b, conv_norm_g, conv_norm_b, w_branch_conv, w_merge_gate, b_merge_gate, w_out, norm_ffn, router_w, router_bias, exp_w_gate, exp_w_up, exp_w_down, shared_w_gate, shared_w_up, shared_w_down, final_norm):
    assert w_ada.shape[0] == 1, "single-layer trunk"
    bp, lp, d = x_prompt.shape
    bs, ls, _ = x_sample.shape
    seqs = [(bp, lp), (bs, ls)]
    n_seq = bp + bs
    c_all = jnp.concatenate([c_prompt, c_sample, jnp.zeros((-n_seq % 8, d), F32)], axis=0)
    params = dict(
        w_ada=w_ada[0], b_ada=b_ada[0], norm_mix=norm_mix[0], w_in=w_in[0],
        ssm_a_re=ssm_a_re[0], ssm_a_im=ssm_a_im[0], ssm_log_dt=ssm_log_dt[0],
        ssm_b_re=ssm_b_re[0], ssm_b_im=ssm_b_im[0], ssm_c_re=ssm_c_re[0], ssm_c_im=ssm_c_im[0],
        ssm_d=ssm_d[0], w_ssm_glu=w_ssm_glu[0], w_branch_ssm=w_branch_ssm[0],
        conv_w=conv_w[0], conv_b=conv_b[0], conv_norm_g=conv_norm_g[0], conv_norm_b=conv_norm_b[0],
        w_branch_conv=w_branch_conv[0], w_merge_gate=w_merge_gate[0], b_merge_gate=b_merge_gate[0],
        w_out=w_out[0], norm_ffn=norm_ffn[0], router_w=router_w[0], router_bias=router_bias[0],
        exp_w_gate=exp_w_gate[0], exp_w_up=exp_w_up[0], exp_w_down=exp_w_down[0],
        shared_w_gate=shared_w_gate[0], shared_w_up=shared_w_up[0], shared_w_down=shared_w_down[0],
        final_norm=final_norm)
    y_a, y_b = _layer(x_prompt.reshape(bp * lp, d), x_sample.reshape(bs * ls, d), c_all, seqs, params)
    return (y_a.reshape(bp, lp, d), y_b.reshape(bs, ls, d))
```

```python
import functools
import math

import jax
import jax.numpy as jnp
import numpy as np
from jax import lax
from jax.experimental import pallas as pl
from jax.experimental.pallas import tpu as pltpu

F32 = jnp.float32
BF16 = jnp.bfloat16
I32 = jnp.int32
U32 = jnp.uint32

SSM_GROUP_CH = 16
SSM_CHUNK = 16
N_ROUTE_GROUPS = 8
TOPK_ROUTE_GROUPS = 4
TOP_K = 8
ROUTED_SCALE = 2.5
RMS_EPS = 1e-6
LN_EPS = 1e-5

V7X_VMEM_LIMIT = 56 * 1024 * 1024
LANES = 128


def _cparams(sem, vmem=V7X_VMEM_LIMIT):
    return pltpu.CompilerParams(dimension_semantics=sem, vmem_limit_bytes=vmem)


def _tile(n, want):
    t = min(n, want)
    while n % t:
        t -= 1
    return t


def _sigmoid(x):
    return 1.0 / (1.0 + jnp.exp(-x))


def _bdot(a, b):
    return jnp.dot(a, b, preferred_element_type=F32)


def _ada_kernel(c_ref, w_ref, b_ref, o_ref):
    c = c_ref[...]
    s = (c * _sigmoid(c)).astype(BF16)
    o_ref[...] = _bdot(s, w_ref[...].astype(BF16)) + b_ref[...]


def _ada(c_all, w_ada, b_ada):
    nseq, d = c_all.shape
    n = w_ada.shape[1]
    tn = _tile(n, 512)
    return pl.pallas_call(
        _ada_kernel,
        out_shape=jax.ShapeDtypeStruct((nseq, n), F32),
        grid=(n // tn,),
        in_specs=[pl.BlockSpec((nseq, d), lambda j: (0, 0)),
                  pl.BlockSpec((d, tn), lambda j: (0, j)),
                  pl.BlockSpec((1, tn), lambda j: (0, j))],
        out_specs=pl.BlockSpec((nseq, tn), lambda j: (0, j)),
        compiler_params=_cparams(("arbitrary",)),
        name="ada",
    )(c_all, w_ada, b_ada.reshape(1, n))


def _pack_bf16_pair(lo, hi):
    lo_b = pltpu.bitcast(lo.astype(BF16).astype(F32), U32) >> 16
    hi_b = pltpu.bitcast(hi.astype(BF16).astype(F32), U32) & jnp.uint32(0xFFFF0000)
    return hi_b | lo_b


def _unpack_bf16_pair(w):
    lo = pltpu.bitcast(w << 16, F32)
    hi = pltpu.bitcast(w & jnp.uint32(0xFFFF0000), F32)
    return lo, hi


def _store_row_slabs(ref, val, pitch):
    rows, width = val.shape
    for j in range(width // LANES):
        ref[pl.ds(j, rows, stride=pitch), :] = val[:, j * LANES:(j + 1) * LANES]


def _load_row_slabs(ref, rows, n_tiles, pitch):
    return jnp.concatenate([ref[pl.ds(j, rows, stride=pitch), :] for j in range(n_tiles)], axis=1)


def _modulated_norm(x, g, sc, sh):
    ms = jnp.mean(x * x, axis=-1, keepdims=True)
    return x * lax.rsqrt(ms + RMS_EPS) * g * (1.0 + sc) + sh


def _two_source_specs(block, n_a, n_inner=0):
    if n_inner:
        first = lambda i, j, *_: (jnp.minimum(i, n_a - 1), jnp.where(i < n_a, j, 0))
        second = lambda i, j, *_: (jnp.maximum(i - n_a, 0), jnp.where(i < n_a, 0, j))
    else:
        first = lambda i, *_: (jnp.minimum(i, n_a - 1), 0)
        second = lambda i, *_: (jnp.maximum(i - n_a, 0), 0)
    return pl.BlockSpec(block, first), pl.BlockSpec(block, second)


def _pick_source(i, n_a, a_ref, b_ref):
    return jnp.where(i < n_a, a_ref[...], b_ref[...])


def _norm1_kernel(seq_ref, xa_ref, xb_ref, g_ref, sc_ref, sh_ref, o_ref, *, n_a):
    del seq_ref
    x = _pick_source(pl.program_id(0), n_a, xa_ref, xb_ref)
    o_ref[...] = _modulated_norm(x, g_ref[...], sc_ref[0], sh_ref[0]).astype(BF16)


def _norm2_kernel(seq_ref, x_ref, g_ref, sc_ref, sh_ref, rw_ref, h_ref, hp_ref, lg_ref):
    del seq_ref
    h = _modulated_norm(x_ref[...], g_ref[...], sc_ref[0], sh_ref[0])
    half = h.shape[1] // 2
    h_ref[...] = h.astype(BF16)
    _store_row_slabs(hp_ref, _pack_bf16_pair(h[:, :half], h[:, half:]), half // LANES)
    lg_ref[...] = jnp.dot(h, rw_ref[...], preferred_element_type=F32,
                          precision=lax.Precision.HIGHEST)


def _norm1(xa, xb, g, sc, sh, seq_of_tile, tm):
    d = xa.shape[1]
    t = xa.shape[0] + xb.shape[0]
    n_a = xa.shape[0] // tm
    row = lambda i, s: (i, 0)
    per_seq = lambda i, s: (s[i], 0, 0)
    return pl.pallas_call(
        functools.partial(_norm1_kernel, n_a=n_a),
        out_shape=jax.ShapeDtypeStruct((t, d), BF16),
        grid_spec=pltpu.PrefetchScalarGridSpec(
            num_scalar_prefetch=1, grid=(t // tm,),
            in_specs=[*_two_source_specs((tm, d), n_a),
                      pl.BlockSpec((1, d), lambda i, s: (0, 0)),
                      pl.BlockSpec((1, 1, d), per_seq),
                      pl.BlockSpec((1, 1, d), per_seq)],
            out_specs=pl.BlockSpec((tm, d), row)),
        compiler_params=_cparams(("arbitrary",)),
        name="norm1",
    )(seq_of_tile, xa, xb, g.reshape(1, d), sc, sh)


def _norm2(x, g, sc, sh, router_w, seq_of_tile, tm):
    t, d = x.shape
    e = router_w.shape[1]
    slab_rows = d // 2 // LANES
    row = lambda i, s: (i, 0)
    per_seq = lambda i, s: (s[i], 0, 0)
    return pl.pallas_call(
        _norm2_kernel,
        out_shape=(jax.ShapeDtypeStruct((t, d), BF16),
                   jax.ShapeDtypeStruct((t * slab_rows, LANES), U32),
                   jax.ShapeDtypeStruct((t, e), F32)),
        grid_spec=pltpu.PrefetchScalarGridSpec(
            num_scalar_prefetch=1, grid=(t // tm,),
            in_specs=[pl.BlockSpec((tm, d), row),
                      pl.BlockSpec((1, d), lambda i, s: (0, 0)),
                      pl.BlockSpec((1, 1, d), per_seq),
                      pl.BlockSpec((1, 1, d), per_seq),
                      pl.BlockSpec((d, e), lambda i, s: (0, 0))],
            out_specs=(pl.BlockSpec((tm, d), row),
                       pl.BlockSpec((tm * slab_rows, LANES), row),
                       pl.BlockSpec((tm, e), row))),
        compiler_params=_cparams(("arbitrary",)),
        name="norm2_router",
    )(seq_of_tile, x, g.reshape(1, d), sc, sh, router_w)


def _proj_kernel(x_ref, w_ref, o_ref):
    o_ref[...] = _bdot(x_ref[...], w_ref[...]).astype(o_ref.dtype)


def _proj(x, w, tm, tn, out_dtype=BF16, name="proj"):
    t, k = x.shape
    n = w.shape[1]
    return pl.pallas_call(
        _proj_kernel,
        out_shape=jax.ShapeDtypeStruct((t, n), out_dtype),
        grid=(t // tm, n // tn),
        in_specs=[pl.BlockSpec((tm, k), lambda i, j: (i, 0)),
                  pl.BlockSpec((k, tn), lambda i, j: (0, j))],
        out_specs=pl.BlockSpec((tm, tn), lambda i, j: (i, j)),
        compiler_params=_cparams(("parallel", "arbitrary")),
        name=name,
    )(x, w)


def _glu_kernel(*refs, silu_first, n_a):
    if n_a is None:
        x_ref, wa_ref, wb_ref, o_ref = refs
        x = x_ref[...].astype(BF16)
    else:
        xa_ref, xb_ref, wa_ref, wb_ref, o_ref = refs
        x = _pick_source(pl.program_id(0), n_a, xa_ref, xb_ref).astype(BF16)
    a = _bdot(x, wa_ref[...])
    b = _bdot(x, wb_ref[...])
    if silu_first:
        y = a * _sigmoid(a) * b
    else:
        y = a * _sigmoid(b)
    o_ref[...] = y.astype(o_ref.dtype)


def _glu_proj(x, wa, wb, tm, tn, silu_first, name, x_tail=None):
    k = x.shape[1]
    n = wa.shape[1]
    if x_tail is None:
        t, n_a = x.shape[0], None
        x_args, x_specs = [x], [pl.BlockSpec((tm, k), lambda i, j: (i, 0))]
    else:
        t, n_a = x.shape[0] + x_tail.shape[0], x.shape[0] // tm
        x_args, x_specs = [x, x_tail], list(_two_source_specs((tm, k), n_a))
    return pl.pallas_call(
        functools.partial(_glu_kernel, silu_first=silu_first, n_a=n_a),
        out_shape=jax.ShapeDtypeStruct((t, n), BF16),
        grid=(t // tm, n // tn),
        in_specs=x_specs + [pl.BlockSpec((k, tn), lambda i, j: (0, j)),
                            pl.BlockSpec((k, tn), lambda i, j: (0, j))],
        out_specs=pl.BlockSpec((tm, tn), lambda i, j: (i, j)),
        compiler_params=_cparams(("parallel", "arbitrary")),
        name=name,
    )(*x_args, wa, wb)


def _merge_kernel(h_ref, a_ref, c_ref, wga_ref, wgb_ref, wa_ref, wc_ref, ba_ref, bb_ref, o_ref):
    h = h_ref[...]
    ga = _sigmoid(_bdot(h, wga_ref[...]) + ba_ref[...])
    gb = _sigmoid(_bdot(h, wgb_ref[...]) + bb_ref[...])
    ya = _bdot(a_ref[...], wa_ref[...])
    yb = _bdot(c_ref[...], wc_ref[...])
    o_ref[...] = (ga * ya + gb * yb).astype(o_ref.dtype)


def _merge(h, a, cv, wga, wgb, wa, wc, ba, bb, tm, tn):
    t, d = h.shape
    ka = a.shape[1]
    kc = cv.shape[1]
    row = lambda i, j: (i, 0)
    col = lambda i, j: (0, j)
    return pl.pallas_call(
        _merge_kernel,
        out_shape=jax.ShapeDtypeStruct((t, d), BF16),
        grid=(t // tm, d // tn),
        in_specs=[pl.BlockSpec((tm, d), row), pl.BlockSpec((tm, ka), row), pl.BlockSpec((tm, kc), row),
                  pl.BlockSpec((d, tn), col), pl.BlockSpec((d, tn), col),
                  pl.BlockSpec((ka, tn), col), pl.BlockSpec((kc, tn), col),
                  pl.BlockSpec((1, tn), col), pl.BlockSpec((1, tn), col)],
        out_specs=pl.BlockSpec((tm, tn), lambda i, j: (i, j)),
        compiler_params=_cparams(("parallel", "arbitrary")),
        name="merge",
    )(h, a, cv, wga, wgb, wa, wc, ba, bb)


def _outproj_kernel(seq_ref, m_ref, w_ref, xa_ref, xb_ref, g_ref, o_ref, *, n_a):
    del seq_ref
    x = _pick_source(pl.program_id(0), n_a, xa_ref, xb_ref)
    o_ref[...] = x + g_ref[0] * _bdot(m_ref[...], w_ref[...])


def _outproj(m, w, xa, xb, gate, seq_of_tile, tm, tn):
    t, k = m.shape
    d = w.shape[1]
    n_a = xa.shape[0] // tm
    return pl.pallas_call(
        functools.partial(_outproj_kernel, n_a=n_a),
        out_shape=jax.ShapeDtypeStruct((t, d), F32),
        grid_spec=pltpu.PrefetchScalarGridSpec(
            num_scalar_prefetch=1, grid=(t // tm, d // tn),
            in_specs=[pl.BlockSpec((tm, k), lambda i, j, s: (i, 0)),
                      pl.BlockSpec((k, tn), lambda i, j, s: (0, j)),
                      *_two_source_specs((tm, tn), n_a, n_inner=1),
                      pl.BlockSpec((1, 1, tn), lambda i, j, s: (s[i], 0, j))],
            out_specs=pl.BlockSpec((tm, tn), lambda i, j, s: (i, j))),
        compiler_params=_cparams(("parallel", "arbitrary")),
        name="outproj",
    )(seq_of_tile, m, w, xa, xb, gate)


def _lagmat_kernel(c_ref, b_ref, o_ref):
    for g in range(c_ref.shape[0]):
        o_ref[g] = jnp.dot(c_ref[g], b_ref[g], preferred_element_type=F32,
                           precision=lax.Precision.HIGHEST)


def _lagmat(cs, bs):
    g2, c, n2 = cs.shape
    w = bs.shape[2]
    gb = _tile(g2, 8)
    return pl.pallas_call(
        _lagmat_kernel,
        out_shape=jax.ShapeDtypeStruct((g2, c, w), F32),
        grid=(g2 // gb,),
        in_specs=[pl.BlockSpec((gb, c, n2), lambda i: (i, 0, 0)),
                  pl.BlockSpec((gb, n2, w), lambda i: (i, 0, 0))],
        out_specs=pl.BlockSpec((gb, c, w), lambda i: (i, 0, 0)),
        compiler_params=_cparams(("arbitrary",)),
        name="ssm_lagmat",
    )(cs, bs)


def _ssm_tables(a_re, a_im, log_dt, b_re, b_im, c_re, c_im, d_skip):
    tc, ch = SSM_CHUNK, SSM_GROUP_CH
    _, g, n = a_re.shape
    dt = jnp.exp(log_dt)[:, :, None]
    zr, zi = a_re * dt, a_im * dt
    ks = jnp.arange(tc + 1, dtype=F32)[:, None, None, None]
    mag = jnp.exp(ks * zr[None])
    pw_re, pw_im = mag * jnp.cos(ks * zi[None]), mag * jnp.sin(ks * zi[None])
    lb_re, lb_im = pw_re[1], pw_im[1]
    den = a_re * a_re + a_im * a_im
    q_re = ((lb_re - 1.0) * a_re + lb_im * a_im) / den
    q_im = (lb_im * a_re - (lb_re - 1.0) * a_im) / den
    bb_re = q_re[..., None] * b_re - q_im[..., None] * b_im
    bb_im = q_re[..., None] * b_im + q_im[..., None] * b_re
    p_re, p_im = pw_re[:tc, ..., None], pw_im[:tc, ..., None]
    bk_re = p_re * bb_re[None] - p_im * bb_im[None]
    bk_im = p_re * bb_im[None] + p_im * bb_re[None]

    bs = jnp.concatenate([bk_re, bk_im], axis=3)
    bs = bs.transpose(1, 2, 3, 0, 4).reshape(2 * g, 2 * n, tc * ch)
    cs = jnp.concatenate([c_re, -c_im], axis=2)
    cs = jnp.broadcast_to(cs[None], (2, g, ch, 2 * n)).reshape(2 * g, ch, 2 * n)
    m = _lagmat(cs, bs).reshape(2, g, ch, tc, ch)

    s_idx = np.arange(tc)[:, None]
    t_idx = np.arange(tc)[None, :]
    lag_f = np.clip(t_idx - s_idx, 0, tc - 1)
    lag_b = np.clip(s_idx - t_idx, 0, tc - 1)
    mf = m[0][:, :, lag_f, :] * jnp.asarray(t_idx >= s_idx, F32)[None, None, :, :, None]
    mb = m[1][:, :, lag_b, :] * jnp.asarray(s_idx >= t_idx, F32)[None, None, :, :, None]
    toep = (mf + mb).transpose(0, 2, 4, 3, 1).reshape(g, tc * ch, tc * ch)

    def rows_sc(x):
        return x.transpose(1, 0, 3, 2).reshape(g, tc * ch, n)
    pin = jnp.concatenate([rows_sc(bk_re[::-1, 0]), rows_sc(bk_re[:, 1]),
                           rows_sc(bk_im[::-1, 0]), rows_sc(bk_im[:, 1])], axis=2)

    def w_tab(pr, pi):
        wr = c_re[None] * pr[:, :, None, :] - c_im[None] * pi[:, :, None, :]
        wi = c_re[None] * pi[:, :, None, :] + c_im[None] * pr[:, :, None, :]
        to_rows = lambda x: x.transpose(1, 3, 0, 2).reshape(g, n, tc * ch)
        return to_rows(wr), to_rows(-wi)
    qf_re, qf_im = w_tab(pw_re[1:tc + 1, 0], pw_im[1:tc + 1, 0])
    qb_re, qb_im = w_tab(pw_re[tc:0:-1, 1], pw_im[tc:0:-1, 1])
    qout = jnp.concatenate([qf_re, qb_re, qf_im, qb_im], axis=1)

    a_step_re = jnp.concatenate([pw_re[tc, 0], pw_re[tc, 1]], axis=1)
    a_step_im = jnp.concatenate([pw_im[tc, 0], pw_im[tc, 1]], axis=1)
    d_tile = jnp.tile(d_skip.reshape(g, 1, ch), (1, tc, 1)).reshape(g, tc * ch)
    return (toep.astype(BF16), pin.astype(BF16), qout.astype(BF16),
            a_step_re, a_step_im, d_tile)


def _gelu_tanh(x):
    return 0.5 * x * (1.0 + jnp.tanh(math.sqrt(2.0 / math.pi) * (x + 0.044715 * (x * x * x))))


def _ssm_kernel(u_ref, t_ref, p_ref, q_ref, are_ref, aim_ref, d_ref, o_ref,
                ug_ref, yg_ref, sin_re, sin_im, hf_re, hf_im, hb_re, hb_im):
    gb, w, _ = t_ref.shape
    tc, ch = SSM_CHUNK, SSM_GROUP_CH
    nc = u_ref.shape[0] // tc
    half = w // 2
    n_col = w // LANES
    slots = LANES // ch
    slot = lax.broadcasted_iota(I32, (8, LANES), 1) // ch
    in_slot = [slot == k for k in range(slots)]

    def to_chunks(rc, _):
        xs = [u_ref[pl.ds(rc * (8 * tc) + t, 8, stride=tc), :] for t in range(tc)]
        cols = [[jnp.zeros((8, LANES), F32) for _ in range(n_col)] for _ in range(gb)]
        for col in range(n_col):
            for s in range(slots):
                v = jnp.zeros((8, LANES), F32)
                for ts in range(slots):
                    v = jnp.where(in_slot[(ts - s) % slots], xs[col * slots + ts], v)
                vr = v if s == 0 else pltpu.roll(v, s * ch, 1)
                for ts in range(slots):
                    g = (ts - s) % slots
                    cols[g][col] = jnp.where(in_slot[ts], vr, cols[g][col])
        for g in range(gb):
            ug_ref[g, pl.ds(rc * 8, 8), :] = jnp.concatenate(cols[g], axis=1)
        return 0

    lax.fori_loop(0, nc // 8, to_chunks, 0)

    for g in range(gb):
        s_in = _bdot(ug_ref[g].astype(BF16), p_ref[g])
        sin_re[pl.ds(g, nc, stride=gb), :] = s_in[:, :half]
        sin_im[pl.ds(g, nc, stride=gb), :] = s_in[:, half:]

    a_re = are_ref[...]
    a_im = aim_ref[...]
    is_fwd = lax.broadcasted_iota(I32, (gb, half), 1) < (half // 2)

    def step(j, carry):
        s_re, s_im = carry
        at_f = pl.ds(pl.multiple_of(j * gb, gb), gb)
        at_b = pl.ds(pl.multiple_of((nc - 1 - j) * gb, gb), gb)
        hf_re[at_f, :] = s_re
        hf_im[at_f, :] = s_im
        hb_re[at_b, :] = s_re
        hb_im[at_b, :] = s_im
        in_re = jnp.where(is_fwd, sin_re[at_f, :], sin_re[at_b, :])
        in_im = jnp.where(is_fwd, sin_im[at_f, :], sin_im[at_b, :])
        return (a_re * s_re - a_im * s_im + in_re, a_re * s_im + a_im * s_re + in_im)

    zero = jnp.zeros((gb, half), F32)
    lax.fori_loop(0, nc, step, (zero, zero))

    fwd_lane = lax.broadcasted_iota(I32, (nc, half), 1) < (half // 2)
    for g in range(gb):
        u = ug_ref[g]
        rows = pl.ds(g, nc, stride=gb)
        hin = jnp.concatenate([jnp.where(fwd_lane, hf_re[rows, :], hb_re[rows, :]),
                               jnp.where(fwd_lane, hf_im[rows, :], hb_im[rows, :])], axis=1)
        y = _bdot(u.astype(BF16), t_ref[g]) + _bdot(hin.astype(BF16), q_ref[g]) + d_ref[pl.ds(g, 1), :] * u
        yg_ref[g] = _gelu_tanh(y)

    def from_chunks(rc, _):
        for col in range(n_col):
            ys = [yg_ref[g, pl.ds(rc * 8, 8), pl.ds(col * LANES, LANES)] for g in range(gb)]
            z = [jnp.zeros((8, LANES), F32) for _ in range(slots)]
            for s in range(slots):
                v = jnp.zeros((8, LANES), F32)
                for g in range(gb):
                    v = jnp.where(in_slot[(g - s) % slots], ys[g], v)
                vr = v if s == 0 else pltpu.roll(v, s * ch, 1)
                for g in range(gb):
                    ts = (g - s) % slots
                    z[ts] = jnp.where(in_slot[g], vr, z[ts])
            for ts in range(slots):
                o_ref[pl.ds(rc * (8 * tc) + col * slots + ts, 8, stride=tc), :] = z[ts]
        return 0

    lax.fori_loop(0, nc // 8, from_chunks, 0)


def _ssm_seqs(u, tabs, row0, n_seq, seq_len):
    toep, pin, qout, a_re, a_im, d_tile = tabs
    _, wch = u.shape
    w = toep.shape[1]
    half = w // 2
    gb = LANES // SSM_GROUP_CH
    nc = seq_len // SSM_CHUNK
    assert nc % 8 == 0 and row0 % seq_len == 0 and wch % LANES == 0
    blk0 = row0 // seq_len
    tab_spec = pl.BlockSpec((gb, w, w), lambda i, b: (i, 0, 0))
    vec_spec = lambda width: pl.BlockSpec((gb, width), lambda i, b: (i, 0))
    return pl.pallas_call(
        _ssm_kernel,
        out_shape=jax.ShapeDtypeStruct((n_seq * seq_len, wch), F32),
        grid=(wch // LANES, n_seq),
        in_specs=[pl.BlockSpec((seq_len, LANES), lambda i, b: (blk0 + b, i)),
                  tab_spec, tab_spec, tab_spec,
                  vec_spec(half), vec_spec(half), vec_spec(w)],
        out_specs=pl.BlockSpec((seq_len, LANES), lambda i, b: (b, i)),
        scratch_shapes=[pltpu.VMEM((gb, nc, w), F32)] * 2 + [pltpu.VMEM((gb * nc, half), F32)] * 6,
        compiler_params=_cparams(("parallel", "arbitrary")),
        name="ssm",
    )(u, toep, pin, qout, a_re, a_im, d_tile)


def _ssm_branch(u, tabs, seqs):
    outs = []
    row0 = 0
    for n_seq, seq_len in seqs:
        outs.append(_ssm_seqs(u, tabs, row0, n_seq, seq_len))
        row0 += n_seq * seq_len
    return outs


def _conv_kernel(first_ref, last_ref, prev_ref, cur_ref, next_ref, w_ref, b_ref, g_ref, beta_ref,
                 o_ref, xs_ref, acc_ref, *, halo, row_chunk):
    i = pl.program_id(0)
    r, c = cur_ref.shape
    kw = w_ref.shape[0]
    half = kw // 2
    n_shift = xs_ref.shape[0]
    keep_prev = jnp.where(first_ref[i] == 0, 1.0, 0.0)
    keep_next = jnp.where(last_ref[i] == 0, 1.0, 0.0)
    xs_ref[0, pl.ds(0, halo), :] = prev_ref[...].astype(F32) * keep_prev
    xs_ref[0, pl.ds(halo, r), :] = cur_ref[...].astype(F32)
    xs_ref[0, pl.ds(halo + r, halo), :] = next_ref[...].astype(F32) * keep_next
    span = r + 2 * halo - n_shift
    for b in range(1, n_shift):
        xs_ref[b, pl.ds(0, span), :] = xs_ref[0, pl.ds(b, span), :]

    col_chunk = _tile(c, 4 * LANES)
    n_cc = c // col_chunk

    def chunk(it, _):
        r0 = pl.multiple_of((it // n_cc) * row_chunk, row_chunk)
        c0 = pl.multiple_of((it % n_cc) * col_chunk, col_chunk)
        cols = pl.ds(c0, col_chunk)
        acc = jnp.zeros((row_chunk, col_chunk), F32)
        for k in range(kw):
            off = halo - half + k
            b = off % n_shift
            acc = acc + w_ref[pl.ds(k, 1), cols] * xs_ref[b, pl.ds(r0 + (off - b), row_chunk), cols]
        acc_ref[pl.ds(r0, row_chunk), cols] = acc
        return 0

    lax.fori_loop(0, (r // row_chunk) * n_cc, chunk, 0)
    ln_rows = _tile(r, 16)

    def norm_rows(it, _):
        rows = pl.ds(pl.multiple_of(it * ln_rows, ln_rows), ln_rows)
        y = acc_ref[rows, :] + b_ref[...]
        mu = jnp.mean(y, axis=-1, keepdims=True)
        yc = y - mu
        var = jnp.mean(yc * yc, axis=-1, keepdims=True)
        z = yc * lax.rsqrt(var + LN_EPS) * g_ref[...] + beta_ref[...]
        o_ref[rows, :] = (z * _sigmoid(z)).astype(o_ref.dtype)
        return 0

    lax.fori_loop(0, r // ln_rows, norm_rows, 0, unroll=2)


def _conv_ln(ub, conv_w, conv_b, ln_g, ln_b, first_of_tile, last_of_tile, r):
    t, c = ub.shape
    kw = conv_w.shape[0]
    halo = 16
    assert kw // 2 <= halo and r % halo == 0
    nb = r // halo
    n_halo_blocks = t // halo
    row_chunk = _tile(r, 32)
    vec = lambda i, f, l: (0, 0)
    return pl.pallas_call(
        functools.partial(_conv_kernel, halo=halo, row_chunk=row_chunk),
        out_shape=jax.ShapeDtypeStruct((t, c), BF16),
        grid_spec=pltpu.PrefetchScalarGridSpec(
            num_scalar_prefetch=2, grid=(t // r,),
            in_specs=[pl.BlockSpec((halo, c), lambda i, f, l: (jnp.maximum(i * nb - 1, 0), 0)),
                      pl.BlockSpec((r, c), lambda i, f, l: (i, 0)),
                      pl.BlockSpec((halo, c), lambda i, f, l: (jnp.minimum((i + 1) * nb, n_halo_blocks - 1), 0)),
                      pl.BlockSpec((kw, c), vec), pl.BlockSpec((1, c), vec),
                      pl.BlockSpec((1, c), vec), pl.BlockSpec((1, c), vec)],
            out_specs=pl.BlockSpec((r, c), lambda i, f, l: (i, 0)),
            scratch_shapes=[pltpu.VMEM((8, r + 2 * halo, c), F32), pltpu.VMEM((r, c), F32)]),
        compiler_params=_cparams(("arbitrary",)),
        name="conv_ln",
    )(first_of_tile, last_of_tile, ub, ub, ub, conv_w, conv_b.reshape(1, c),
      ln_g.reshape(1, c), ln_b.reshape(1, c))


def _seg_allreduce(x, lane, seg, op):
    n = x.shape[-1]
    s = 1
    while s < seg:
        up = pltpu.roll(x, n - s, 1)
        dn = pltpu.roll(x, s, 1)
        x = op(x, jnp.where((lane & s) == 0, up, dn))
        s *= 2
    return x


def _route_kernel(lg_ref, bias_ref, idx_ref, wt_ref, rank_ref, cnt_ref, carry_ref, *, n_groups, topk_groups, top_k):
    i = pl.program_id(0)
    tm, e = lg_ref.shape
    seg = e // n_groups
    lane = lax.broadcasted_iota(I32, (tm, e), 1)
    lane_f = lane.astype(F32)
    neg = jnp.float32(-jnp.inf)

    scores = _sigmoid(lg_ref[...])
    choice = scores + bias_ref[...]

    m1 = _seg_allreduce(choice, lane, seg, jnp.maximum)
    first = _seg_allreduce(jnp.where(choice == m1, lane_f, float(e)), lane, seg, jnp.minimum)
    m2 = _seg_allreduce(jnp.where(lane_f == first, neg, choice), lane, seg, jnp.maximum)
    gscore = m1 + m2

    beaten = jnp.zeros((tm, e), F32)
    for dshift in range(1, n_groups):
        other = pltpu.roll(gscore, dshift * seg, 1)
        other_is_lower = lane >= dshift * seg
        beaten = beaten + jnp.where(other_is_lower, jnp.where(other >= gscore, 1.0, 0.0),
                                    jnp.where(other > gscore, 1.0, 0.0))
    masked = jnp.where(beaten < topk_groups, choice, neg)

    idx_out = jnp.zeros((tm, e), F32)
    sc_out = jnp.zeros((tm, e), F32)
    sel = jnp.zeros((tm, e), F32)
    picks = []
    for k in range(top_k):
        m = jnp.max(masked, axis=1, keepdims=True)
        pick = jnp.min(jnp.where(masked == m, lane_f, float(e)), axis=1, keepdims=True)
        hit = lane_f == pick
        s_k = jnp.sum(jnp.where(hit, scores, 0.0), axis=1, keepdims=True)
        idx_out = jnp.where(lane == k, pick, idx_out)
        sc_out = jnp.where(lane == k, s_k, sc_out)
        sel = jnp.where(hit, 1.0, sel)
        masked = jnp.where(hit, neg, masked)
        picks.append(hit)
    denom = jnp.sum(sc_out, axis=1, keepdims=True)
    idx_ref[...] = idx_out.astype(I32)
    wt_ref[...] = sc_out / denom * ROUTED_SCALE

    @pl.when(i == 0)
    def _():
        carry_ref[...] = jnp.zeros_like(carry_ref)

    rows = lax.broadcasted_iota(I32, (tm, tm), 0)
    cols = lax.broadcasted_iota(I32, (tm, tm), 1)
    tri = jnp.where(cols < rows, 1.0, 0.0).astype(BF16)
    rank = _bdot(tri, sel.astype(BF16)) + carry_ref[...]
    rank_out = jnp.zeros((tm, e), F32)
    for k in range(top_k):
        r_k = jnp.sum(jnp.where(picks[k], rank, 0.0), axis=1, keepdims=True)
        rank_out = jnp.where(lane == k, r_k, rank_out)
    rank_ref[...] = rank_out.astype(I32)
    carry_ref[...] = carry_ref[...] + jnp.sum(sel, axis=0, keepdims=True)
    cnt_ref[...] = carry_ref[...].astype(I32)


def _route(logits, bias, tm):
    t, e = logits.shape
    row = lambda i: (i, 0)
    fixed = lambda i: (0, 0)
    return pl.pallas_call(
        functools.partial(_route_kernel, n_groups=N_ROUTE_GROUPS, topk_groups=TOPK_ROUTE_GROUPS, top_k=TOP_K),
        out_shape=(jax.ShapeDtypeStruct((t, e), I32), jax.ShapeDtypeStruct((t, e), F32),
                   jax.ShapeDtypeStruct((t, e), I32), jax.ShapeDtypeStruct((1, e), I32)),
        grid=(t // tm,),
        in_specs=[pl.BlockSpec((tm, e), row), pl.BlockSpec((1, e), fixed)],
        out_specs=(pl.BlockSpec((tm, e), row), pl.BlockSpec((tm, e), row),
                   pl.BlockSpec((tm, e), row), pl.BlockSpec((1, e), fixed)),
        scratch_shapes=[pltpu.VMEM((1, e), F32)],
        compiler_params=_cparams(("arbitrary",)),
        name="route",
    )(logits, bias.reshape(1, e))


def _dest_kernel(idx_ref, rank_ref, start_ref, o_ref, *, top_k):
    tm, e = idx_ref.shape
    lane = lax.broadcasted_iota(I32, (tm, e), 1)
    idx = idx_ref[...]
    start = start_ref[...].astype(F32)
    out = jnp.zeros((tm, e), F32)
    for k in range(top_k):
        s_k = jnp.sum(jnp.where(lane == idx[:, k:k + 1], start, 0.0), axis=1, keepdims=True)
        out = jnp.where(lane == k, s_k, out)
    o_ref[...] = out.astype(I32) + rank_ref[...]


def _dest_rows(top_idx, rank, pad_start, tm):
    t, e = top_idx.shape
    row = lambda i: (i, 0)
    return pl.pallas_call(
        functools.partial(_dest_kernel, top_k=TOP_K),
        out_shape=jax.ShapeDtypeStruct((t, e), I32),
        grid=(t // tm,),
        in_specs=[pl.BlockSpec((tm, e), row), pl.BlockSpec((tm, e), row), pl.BlockSpec((1, e), lambda i: (0, 0))],
        out_specs=pl.BlockSpec((tm, e), row),
        compiler_params=_cparams(("arbitrary",)),
        name="dest_rows",
    )(top_idx, rank, pad_start.reshape(1, e))


def _expert_kernel(blk_e_ref, nblk_ref, idx_hbm, x_hbm, wg_ref, wu_ref, wd_ref, ys_hbm,
                   idx_smem, xb0, xb1, ob0, ob1, sem_idx, sem_g, sem_s):
    del blk_e_ref
    b = pl.program_id(0)
    n = nblk_ref[0]
    kh = wg_ref.shape[1] // 2
    ns = kh // LANES
    pitch = ns + 1
    r = xb0.shape[0] // pitch
    xbufs = (xb0, xb1)
    obufs = (ob0, ob1)
    dump0 = ys_hbm.shape[0] // ns - 2 * r

    def idx_copy(blk, s):
        return pltpu.make_async_copy(idx_hbm.at[blk], idx_smem.at[s], sem_idx.at[s])

    def hbm_row(ref, row):
        return ref.at[pl.ds(pl.multiple_of(row * ns, ns), ns)]

    def vmem_row(ref, i):
        return ref.at[pl.ds(i * pitch, ns)]

    def gather_start(s):
        for i in range(r):
            pltpu.make_async_copy(hbm_row(x_hbm, idx_smem[s, 0, i]), vmem_row(xbufs[s], i),
                                  sem_g.at[s]).start(priority=1)

    def gather_wait(s):
        for i in range(r):
            pltpu.make_async_copy(hbm_row(x_hbm, 0), vmem_row(xbufs[s], i), sem_g.at[s]).wait()

    def scatter_start(s):
        for i in range(r):
            pltpu.make_async_copy(vmem_row(obufs[s], i), hbm_row(ys_hbm, idx_smem[s, 1, i]),
                                  sem_s.at[s]).start(priority=i % 2)

    def scatter_wait(s):
        for i in range(r):
            pltpu.make_async_copy(vmem_row(obufs[s], i), hbm_row(ys_hbm, 0), sem_s.at[s]).wait()

    @pl.when(b == 0)
    def _():
        ob1[...] = jnp.zeros_like(ob1)
        for h in range(2):
            fill = pltpu.make_async_copy(ob1.at[pl.ds(0, r * ns)],
                                         ys_hbm.at[pl.ds((dump0 + h * r) * ns, r * ns)], sem_s.at[1])
            fill.start()
            fill.wait()
        first = idx_copy(0, 0)
        first.start()
        first.wait()
        gather_start(0)

        @pl.when(n > 1)
        def _():
            idx_copy(1, 1).start()

    def step(cur):
        nxt = 1 - cur
        gather_wait(cur)

        @pl.when(b + 1 < n)
        def _():
            idx_copy(b + 1, nxt).wait()
            gather_start(nxt)

        @pl.when(b >= 2)
        def _():
            scatter_wait(cur)

        lo, hi = _unpack_bf16_pair(_load_row_slabs(xbufs[cur], r, ns, pitch))
        lo = lo.astype(BF16)
        hi = hi.astype(BF16)
        gate = _bdot(lo, wg_ref[0, pl.ds(0, kh), :]) + _bdot(hi, wg_ref[0, pl.ds(kh, kh), :])
        up = _bdot(lo, wu_ref[0, pl.ds(0, kh), :]) + _bdot(hi, wu_ref[0, pl.ds(kh, kh), :])
        act = (gate * _sigmoid(gate) * up).astype(BF16)
        out = _bdot(act, wd_ref[0])
        _store_row_slabs(obufs[cur], _pack_bf16_pair(out[:, :kh], out[:, kh:]), pitch)
        scatter_start(cur)

        @pl.when(b + 2 < n)
        def _():
            idx_copy(b + 2, cur).start()

        @pl.when(b == n - 1)
        def _():
            scatter_wait(cur)

            @pl.when(b >= 1)
            def _():
                scatter_wait(nxt)

    for parity in range(2):
        @pl.when((b < n) & ((b & 1) == parity))
        def _(parity=parity):
            step(parity)


def _experts(blk_e, nblk, row_idx, xp, wg, wu, wd, n_out_rows):
    n_blocks, _, r = row_idx.shape
    n_e, d, de = wg.shape
    ns = d // 2 // LANES
    last = lambda b, be, nb: jnp.minimum(b, nb[0] - 1)
    return pl.pallas_call(
        _expert_kernel,
        out_shape=jax.ShapeDtypeStruct(((n_out_rows + 2 * r) * ns, LANES), U32),
        grid_spec=pltpu.PrefetchScalarGridSpec(
            num_scalar_prefetch=2, grid=(n_blocks,),
            in_specs=[pl.BlockSpec(memory_space=pl.ANY),
                      pl.BlockSpec(memory_space=pl.ANY),
                      pl.BlockSpec((1, d, de), lambda b, be, nb: (be[last(b, be, nb)], 0, 0)),
                      pl.BlockSpec((1, d, de), lambda b, be, nb: (be[last(b, be, nb)], 0, 0)),
                      pl.BlockSpec((1, de, d), lambda b, be, nb: (be[last(b, be, nb)], 0, 0))],
            out_specs=pl.BlockSpec(memory_space=pl.ANY),
            scratch_shapes=[pltpu.SMEM((2, 2, r), I32)] + [pltpu.VMEM((r * (ns + 1), LANES), U32)] * 4
                           + [pltpu.SemaphoreType.DMA((2,))] * 3),
        compiler_params=_cparams(("arbitrary",)),
        name="experts",
    )(blk_e, nblk, row_idx, xp, wg, wu, wd)


def _combine_kernel(seq_ref, *refs, top_k, n_a):
    del seq_ref
    y_refs = refs[:top_k]
    wt_ref, act_ref, wsd_ref, x_ref, g2_ref, fn_ref, oa_ref, ob_ref, routed_ref = refs[top_k:]
    i = pl.program_id(0)
    tm, d = x_ref.shape
    kh = d // 2
    ns = kh // LANES
    wt = wt_ref[...]
    w_cols = [wt[:, k:k + 1] for k in range(top_k)]
    for j in range(ns):
        r_lo = jnp.zeros((tm, LANES), F32)
        r_hi = jnp.zeros((tm, LANES), F32)
        for k in range(top_k):
            lo, hi = _unpack_bf16_pair(y_refs[k][pl.ds(j, tm, stride=ns), :])
            r_lo = r_lo + w_cols[k] * lo
            r_hi = r_hi + w_cols[k] * hi
        routed_ref[:, pl.ds(j * LANES, LANES)] = r_lo
        routed_ref[:, pl.ds(kh + j * LANES, LANES)] = r_hi
    shared = _bdot(act_ref[...], wsd_ref[...])
    y = x_ref[...] + g2_ref[0] * (shared + routed_ref[...])
    ms = jnp.mean(y * y, axis=-1, keepdims=True)
    out = y * lax.rsqrt(ms + RMS_EPS) * fn_ref[...]

    @pl.when(i < n_a)
    def _():
        oa_ref[...] = out

    @pl.when(i >= n_a)
    def _():
        ob_ref[...] = out


def _combine(ys, wt, act, wsd, x1, g2, final_norm, seq_of_tile, tm, rows_a):
    t, d = x1.shape
    n_a = rows_a // tm
    e = wt.shape[1]
    ds_ = act.shape[1]
    ns = d // 2 // LANES
    nt = t // tm
    row = lambda i, s: (i, 0)
    fixed = lambda i, s: (0, 0)
    y_specs = [pl.BlockSpec((tm * ns, LANES), functools.partial(lambda i, s, k: (k * nt + i, 0), k=k))
               for k in range(TOP_K)]
    return pl.pallas_call(
        functools.partial(_combine_kernel, top_k=TOP_K, n_a=n_a),
        out_shape=(jax.ShapeDtypeStruct((rows_a, d), F32), jax.ShapeDtypeStruct((t - rows_a, d), F32)),
        grid_spec=pltpu.PrefetchScalarGridSpec(
            num_scalar_prefetch=1, grid=(nt,),
            in_specs=y_specs + [pl.BlockSpec((tm, e), row),
                                pl.BlockSpec((tm, ds_), row),
                                pl.BlockSpec((ds_, d), fixed),
                                pl.BlockSpec((tm, d), row),
                                pl.BlockSpec((1, 1, d), lambda i, s: (s[i], 0, 0)),
                                pl.BlockSpec((1, d), fixed)],
            out_specs=_two_source_specs((tm, d), n_a),
            scratch_shapes=[pltpu.VMEM((tm, d), F32)]),
        compiler_params=_cparams(("arbitrary",)),
        name="combine",
    )(seq_of_tile, *([ys] * TOP_K), wt, act, wsd, x1, g2, final_norm.reshape(1, d))


def _seq_tables(seqs, tile):
    sid, first, last = [], [], []
    s = 0
    for n_seq, seq_len in seqs:
        per = seq_len // tile
        for _ in range(n_seq):
            sid += [s] * per
            first += [1] + [0] * (per - 1)
            last += [0] * (per - 1) + [1]
            s += 1
    as_i32 = lambda v: jnp.asarray(np.asarray(v, np.int32))
    return as_i32(sid), as_i32(first), as_i32(last)


def _layer(xa, xb, c_all, seqs, p):
    d = xa.shape[1]
    t = xa.shape[0] + xb.shape[0]
    w_ssm = p["ssm_d"].shape[0]
    w_conv = p["conv_b"].shape[0]
    n_e = p["router_w"].shape[1]
    min_seq = min(sl for _, sl in seqs)

    tm = _tile(min_seq, 1024)
    tn_row = _tile(min_seq, 256)
    seq_mm, _, _ = _seq_tables(seqs, tm)
    seq_nr, _, _ = _seq_tables(seqs, tn_row)

    n_seq_total = c_all.shape[0]
    ada = _ada(c_all, p["w_ada"], p["b_ada"])
    sh1, sc1, g1, sh2, sc2, g2 = [a.reshape(n_seq_total, 1, d) for a in jnp.split(ada, 6, axis=1)]

    bf = lambda w: w.astype(BF16)
    w_in = p["w_in"]
    h = _norm1(xa, xb, p["norm_mix"], sc1, sh1, seq_nr, tn_row)

    u_ssm = _proj(h, bf(w_in[:, :w_ssm]), tm, _tile(w_ssm, 1024), out_dtype=F32, name="in_ssm")
    tabs = _ssm_tables(p["ssm_a_re"], p["ssm_a_im"], p["ssm_log_dt"], p["ssm_b_re"], p["ssm_b_im"],
                       p["ssm_c_re"], p["ssm_c_im"], p["ssm_d"])
    ya_a, ya_b = _ssm_branch(u_ssm, tabs, seqs)
    wgl = p["w_ssm_glu"]
    a_act = _glu_proj(ya_a, bf(wgl[:, :w_ssm]), bf(wgl[:, w_ssm:]), _tile(min_seq, 512), _tile(w_ssm, 1024), False,
                      "ssm_glu", x_tail=ya_b)

    ub = _glu_proj(h, bf(w_in[:, w_ssm:w_ssm + w_conv]), bf(w_in[:, w_ssm + w_conv:]),
                   tm, _tile(w_conv, 512), False, "in_conv_glu")
    r_conv = _tile(min_seq, 256)
    _, first_c, last_c = _seq_tables(seqs, r_conv)
    cv = _conv_ln(ub, p["conv_w"], p["conv_b"], p["conv_norm_g"], p["conv_norm_b"], first_c, last_c, r_conv)

    wmg = p["w_merge_gate"]
    bmg = p["b_merge_gate"]
    tm_merge = _tile(min_seq, 512)
    m = _merge(h, a_act, cv, bf(wmg[:, :d]), bf(wmg[:, d:]), bf(p["w_branch_ssm"]), bf(p["w_branch_conv"]),
               bmg[:d].reshape(1, d), bmg[d:].reshape(1, d), tm_merge, _tile(d, 512))
    x1 = _outproj(m, bf(p["w_out"]), xa, xb, g1, seq_mm, tm, _tile(d, 512))

    h2, h2p, logits = _norm2(x1, p["norm_ffn"], sc2, sh2, p["router_w"], seq_nr, tn_row)
    tm_route = _tile(t, 256)
    top_idx, top_w, rank, counts = _route(logits, p["router_bias"], tm_route)

    r_blk = 256 if t * TOP_K >= 256 * n_e else 8
    counts = counts[0]
    padded = (counts + r_blk - 1) // r_blk * r_blk
    pad_end = jnp.cumsum(padded)
    pad_start = pad_end - padded
    n_blocks = -(-(t * TOP_K) // r_blk) + n_e
    dest = _dest_rows(top_idx, rank, pad_start, tm_route)[:, :TOP_K]
    pos = jnp.arange(n_blocks * r_blk, dtype=I32)
    dump = TOP_K * t + ((pos // r_blk) % 2) * r_blk + pos % r_blk
    out_rows = jnp.arange(TOP_K, dtype=I32)[None, :] * t + jnp.arange(t, dtype=I32)[:, None]
    row_dst = dump.at[dest.reshape(-1)].set(out_rows.reshape(-1), unique_indices=True, mode="promise_in_bounds")
    row_src = jnp.where(row_dst < TOP_K * t, row_dst % t, 0)
    row_idx = jnp.stack([row_src.reshape(n_blocks, r_blk), row_dst.reshape(n_blocks, r_blk)], axis=1)
    blk_start = jnp.arange(n_blocks, dtype=I32) * r_blk
    blk_e = jnp.minimum(jnp.searchsorted(pad_end, blk_start, side="right"), n_e - 1).astype(I32)
    nblk = (pad_end[-1] // r_blk).astype(I32).reshape(1)

    ys = _experts(blk_e, nblk, row_idx, h2p, bf(p["exp_w_gate"]), bf(p["exp_w_up"]), bf(p["exp_w_down"]),
                  TOP_K * t)

    act = _glu_proj(h2, bf(p["shared_w_gate"]), bf(p["shared_w_up"]), tm, p["shared_w_gate"].shape[1],
                    True, "shared_act")
    tm_c = _tile(min_seq, 128)
    seq_c, _, _ = _seq_tables(seqs, tm_c)
    return _combine(ys, top_w, act, bf(p["shared_w_down"]), x1, g2, p["final_norm"], seq_c, tm_c, xa.shape[0])


def kernel(x_prompt, x_sample, c_prompt, c_sample, w_ada, b_ada, norm_mix, w_in, ssm_a_re, ssm_a_im, ssm_log_dt, ssm_b_re, ssm_b_im, ssm_c_re, ssm_c_im, ssm_d, w_ssm_glu, w_branch_ssm, conv_w, conv_b, conv_norm_g, conv_norm_b, w_branch_conv, w_merge_gate, b_merge_gate, w_out, norm_ffn, router_w, router_bias, exp_w_gate, exp_w_up, exp_w_down, shared_w_gate, shared_w_up, shared_w_down, final_norm):
    assert w_ada.shape[0] == 1, "single-layer trunk"
    bp, lp, d = x_prompt.shape
    bs, ls, _ = x_sample.shape
    seqs = [(bp, lp), (bs, ls)]
    n_seq = bp + bs
    c_all = jnp.concatenate([c_prompt, c_sample, jnp.zeros((-n_seq % 8, d), F32)], axis=0)
    params = dict(
        w_ada=w_ada[0], b_ada=b_ada[0], norm_mix=norm_mix[0], w_in=w_in[0],
        ssm_a_re=ssm_a_re[0], ssm_a_im=ssm_a_im[0], ssm_log_dt=ssm_log_dt[0],
        ssm_b_re=ssm_b_re[0], ssm_b_im=ssm_b_im[0], ssm_c_re=ssm_c_re[0], ssm_c_im=ssm_c_im[0],
        ssm_d=ssm_d[0], w_ssm_glu=w_ssm_glu[0], w_branch_ssm=w_branch_ssm[0],
        conv_w=conv_w[0], conv_b=conv_b[0], conv_norm_g=conv_norm_g[0], conv_norm_b=conv_norm_b[0],
        w_branch_conv=w_branch_conv[0], w_merge_gate=w_merge_gate[0], b_merge_gate=b_merge_gate[0],
        w_out=w_out[0], norm_ffn=norm_ffn[0], router_w=router_w[0], router_bias=router_bias[0],
        exp_w_gate=exp_w_gate[0], exp_w_up=exp_w_up[0], exp_w_down=exp_w_down[0],
        shared_w_gate=shared_w_gate[0], shared_w_up=shared_w_up[0], shared_w_down=shared_w_down[0],
        final_norm=final_norm)
    y_a, y_b = _layer(x_prompt.reshape(bp * lp, d), x_sample.reshape(bs * ls, d), c_all, seqs, params)
    return (y_a.reshape(bp, lp, d), y_b.reshape(bs, ls, d))
```

```python
import functools
import math

import jax
import jax.numpy as jnp
import numpy as np
from jax import lax
from jax.experimental import pallas as pl
from jax.experimental.pallas import tpu as pltpu

F32 = jnp.float32
BF16 = jnp.bfloat16
I32 = jnp.int32
U32 = jnp.uint32

SSM_GROUP_CH = 16
SSM_CHUNK = 16
N_ROUTE_GROUPS = 8
TOPK_ROUTE_GROUPS = 4
TOP_K = 8
ROUTED_SCALE = 2.5
RMS_EPS = 1e-6
LN_EPS = 1e-5

V7X_VMEM_LIMIT = 56 * 1024 * 1024
LANES = 128


def _cparams(sem, vmem=V7X_VMEM_LIMIT):
    return pltpu.CompilerParams(dimension_semantics=sem, vmem_limit_bytes=vmem)


def _tile(n, want):
    t = min(n, want)
    while n % t:
        t -= 1
    return t


def _sigmoid(x):
    return 1.0 / (1.0 + jnp.exp(-x))


def _bdot(a, b):
    return jnp.dot(a, b, preferred_element_type=F32)


def _ada_kernel(c_ref, w_ref, b_ref, o_ref):
    c = c_ref[...]
    s = (c * _sigmoid(c)).astype(BF16)
    o_ref[...] = _bdot(s, w_ref[...].astype(BF16)) + b_ref[...]


def _ada(c_all, w_ada, b_ada):
    nseq, d = c_all.shape
    n = w_ada.shape[1]
    tn = _tile(n, 512)
    return pl.pallas_call(
        _ada_kernel,
        out_shape=jax.ShapeDtypeStruct((nseq, n), F32),
        grid=(n // tn,),
        in_specs=[pl.BlockSpec((nseq, d), lambda j: (0, 0)),
                  pl.BlockSpec((d, tn), lambda j: (0, j)),
                  pl.BlockSpec((1, tn), lambda j: (0, j))],
        out_specs=pl.BlockSpec((nseq, tn), lambda j: (0, j)),
        compiler_params=_cparams(("arbitrary",)),
        name="ada",
    )(c_all, w_ada, b_ada.reshape(1, n))


def _pack_bf16_pair(lo, hi):
    lo_b = pltpu.bitcast(lo.astype(BF16).astype(F32), U32) >> 16
    hi_b = pltpu.bitcast(hi.astype(BF16).astype(F32), U32) & jnp.uint32(0xFFFF0000)
    return hi_b | lo_b


def _unpack_bf16_pair(w):
    lo = pltpu.bitcast(w << 16, F32)
    hi = pltpu.bitcast(w & jnp.uint32(0xFFFF0000), F32)
    return lo, hi


def _store_row_slabs(ref, val, pitch):
    rows, width = val.shape
    for j in range(width // LANES):
        ref[pl.ds(j, rows, stride=pitch), :] = val[:, j * LANES:(j + 1) * LANES]


def _load_row_slabs(ref, rows, n_tiles, pitch):
    return jnp.concatenate([ref[pl.ds(j, rows, stride=pitch), :] for j in range(n_tiles)], axis=1)


def _modulated_norm(x, g, sc, sh):
    ms = jnp.mean(x * x, axis=-1, keepdims=True)
    return x * lax.rsqrt(ms + RMS_EPS) * g * (1.0 + sc) + sh


def _two_source_specs(block, n_a, n_inner=0):
    if n_inner:
        first = lambda i, j, *_: (jnp.minimum(i, n_a - 1), jnp.where(i < n_a, j, 0))
        second = lambda i, j, *_: (jnp.maximum(i - n_a, 0), jnp.where(i < n_a, 0, j))
    else:
        first = lambda i, *_: (jnp.minimum(i, n_a - 1), 0)
        second = lambda i, *_: (jnp.maximum(i - n_a, 0), 0)
    return pl.BlockSpec(block, first), pl.BlockSpec(block, second)


def _pick_source(i, n_a, a_ref, b_ref):
    return jnp.where(i < n_a, a_ref[...], b_ref[...])


def _norm1_kernel(seq_ref, xa_ref, xb_ref, g_ref, sc_ref, sh_ref, o_ref, *, n_a):
    del seq_ref
    x = _pick_source(pl.program_id(0), n_a, xa_ref, xb_ref)
    o_ref[...] = _modulated_norm(x, g_ref[...], sc_ref[0], sh_ref[0]).astype(BF16)


def _norm2_kernel(seq_ref, x_ref, g_ref, sc_ref, sh_ref, rw_ref, h_ref, hp_ref, lg_ref):
    del seq_ref
    h = _modulated_norm(x_ref[...], g_ref[...], sc_ref[0], sh_ref[0])
    half = h.shape[1] // 2
    h_ref[...] = h.astype(BF16)
    _store_row_slabs(hp_ref, _pack_bf16_pair(h[:, :half], h[:, half:]), half // LANES)
    lg_ref[...] = jnp.dot(h, rw_ref[...], preferred_element_type=F32,
                          precision=lax.Precision.HIGHEST)


def _norm1(xa, xb, g, sc, sh, seq_of_tile, tm):
    d = xa.shape[1]
    t = xa.shape[0] + xb.shape[0]
    n_a = xa.shape[0] // tm
    row = lambda i, s: (i, 0)
    per_seq = lambda i, s: (s[i], 0, 0)
    return pl.pallas_call(
        functools.partial(_norm1_kernel, n_a=n_a),
        out_shape=jax.ShapeDtypeStruct((t, d), BF16),
        grid_spec=pltpu.PrefetchScalarGridSpec(
            num_scalar_prefetch=1, grid=(t // tm,),
            in_specs=[*_two_source_specs((tm, d), n_a),
                      pl.BlockSpec((1, d), lambda i, s: (0, 0)),
                      pl.BlockSpec((1, 1, d), per_seq),
                      pl.BlockSpec((1, 1, d), per_seq)],
            out_specs=pl.BlockSpec((tm, d), row)),
        compiler_params=_cparams(("arbitrary",)),
        name="norm1",
    )(seq_of_tile, xa, xb, g.reshape(1, d), sc, sh)


def _norm2(x, g, sc, sh, router_w, seq_of_tile, tm):
    t, d = x.shape
    e = router_w.shape[1]
    slab_rows = d // 2 // LANES
    row = lambda i, s: (i, 0)
    per_seq = lambda i, s: (s[i], 0, 0)
    return pl.pallas_call(
        _norm2_kernel,
        out_shape=(jax.ShapeDtypeStruct((t, d), BF16),
                   jax.ShapeDtypeStruct((t * slab_rows, LANES), U32),
                   jax.ShapeDtypeStruct((t, e), F32)),
        grid_spec=pltpu.PrefetchScalarGridSpec(
            num_scalar_prefetch=1, grid=(t // tm,),
            in_specs=[pl.BlockSpec((tm, d), row),
                      pl.BlockSpec((1, d), lambda i, s: (0, 0)),
                      pl.BlockSpec((1, 1, d), per_seq),
                      pl.BlockSpec((1, 1, d), per_seq),
                      pl.BlockSpec((d, e), lambda i, s: (0, 0))],
            out_specs=(pl.BlockSpec((tm, d), row),
                       pl.BlockSpec((tm * slab_rows, LANES), row),
                       pl.BlockSpec((tm, e), row))),
        compiler_params=_cparams(("arbitrary",)),
        name="norm2_router",
    )(seq_of_tile, x, g.reshape(1, d), sc, sh, router_w)


def _proj_kernel(x_ref, w_ref, o_ref):
    o_ref[...] = _bdot(x_ref[...], w_ref[...]).astype(o_ref.dtype)


def _proj(x, w, tm, tn, out_dtype=BF16, name="proj"):
    t, k = x.shape
    n = w.shape[1]
    return pl.pallas_call(
        _proj_kernel,
        out_shape=jax.ShapeDtypeStruct((t, n), out_dtype),
        grid=(t // tm, n // tn),
        in_specs=[pl.BlockSpec((tm, k), lambda i, j: (i, 0)),
                  pl.BlockSpec((k, tn), lambda i, j: (0, j))],
        out_specs=pl.BlockSpec((tm, tn), lambda i, j: (i, j)),
        compiler_params=_cparams(("parallel", "arbitrary")),
        name=name,
    )(x, w)


def _glu_kernel(*refs, silu_first, n_a):
    if n_a is None:
        x_ref, wa_ref, wb_ref, o_ref = refs
        x = x_ref[...].astype(BF16)
    else:
        xa_ref, xb_ref, wa_ref, wb_ref, o_ref = refs
        x = _pick_source(pl.program_id(0), n_a, xa_ref, xb_ref).astype(BF16)
    a = _bdot(x, wa_ref[...])
    b = _bdot(x, wb_ref[...])
    if silu_first:
        y = a * _sigmoid(a) * b
    else:
        y = a * _sigmoid(b)
    o_ref[...] = y.astype(o_ref.dtype)


def _glu_proj(x, wa, wb, tm, tn, silu_first, name, x_tail=None):
    k = x.shape[1]
    n = wa.shape[1]
    if x_tail is None:
        t, n_a = x.shape[0], None
        x_args, x_specs = [x], [pl.BlockSpec((tm, k), lambda i, j: (i, 0))]
    else:
        t, n_a = x.shape[0] + x_tail.shape[0], x.shape[0] // tm
        x_args, x_specs = [x, x_tail], list(_two_source_specs((tm, k), n_a))
    return pl.pallas_call(
        functools.partial(_glu_kernel, silu_first=silu_first, n_a=n_a),
        out_shape=jax.ShapeDtypeStruct((t, n), BF16),
        grid=(t // tm, n // tn),
        in_specs=x_specs + [pl.BlockSpec((k, tn), lambda i, j: (0, j)),
                            pl.BlockSpec((k, tn), lambda i, j: (0, j))],
        out_specs=pl.BlockSpec((tm, tn), lambda i, j: (i, j)),
        compiler_params=_cparams(("parallel", "arbitrary")),
        name=name,
    )(*x_args, wa, wb)


def _merge_kernel(h_ref, a_ref, c_ref, wga_ref, wgb_ref, wa_ref, wc_ref, ba_ref, bb_ref, o_ref):
    h = h_ref[...]
    ga = _sigmoid(_bdot(h, wga_ref[...]) + ba_ref[...])
    gb = _sigmoid(_bdot(h, wgb_ref[...]) + bb_ref[...])
    ya = _bdot(a_ref[...], wa_ref[...])
    yb = _bdot(c_ref[...], wc_ref[...])
    o_ref[...] = (ga * ya + gb * yb).astype(o_ref.dtype)


def _merge(h, a, cv, wga, wgb, wa, wc, ba, bb, tm, tn):
    t, d = h.shape
    ka = a.shape[1]
    kc = cv.shape[1]
    row = lambda i, j: (i, 0)
    col = lambda i, j: (0, j)
    return pl.pallas_call(
        _merge_kernel,
        out_shape=jax.ShapeDtypeStruct((t, d), BF16),
        grid=(t // tm, d // tn),
        in_specs=[pl.BlockSpec((tm, d), row), pl.BlockSpec((tm, ka), row), pl.BlockSpec((tm, kc), row),
                  pl.BlockSpec((d, tn), col), pl.BlockSpec((d, tn), col),
                  pl.BlockSpec((ka, tn), col), pl.BlockSpec((kc, tn), col),
                  pl.BlockSpec((1, tn), col), pl.BlockSpec((1, tn), col)],
        out_specs=pl.BlockSpec((tm, tn), lambda i, j: (i, j)),
        compiler_params=_cparams(("parallel", "arbitrary")),
        name="merge",
    )(h, a, cv, wga, wgb, wa, wc, ba, bb)


def _outproj_kernel(seq_ref, m_ref, w_ref, xa_ref, xb_ref, g_ref, o_ref, *, n_a):
    del seq_ref
    x = _pick_source(pl.program_id(0), n_a, xa_ref, xb_ref)
    o_ref[...] = x + g_ref[0] * _bdot(m_ref[...], w_ref[...])


def _outproj(m, w, xa, xb, gate, seq_of_tile, tm, tn):
    t, k = m.shape
    d = w.shape[1]
    n_a = xa.shape[0] // tm
    return pl.pallas_call(
        functools.partial(_outproj_kernel, n_a=n_a),
        out_shape=jax.ShapeDtypeStruct((t, d), F32),
        grid_spec=pltpu.PrefetchScalarGridSpec(
            num_scalar_prefetch=1, grid=(t // tm, d // tn),
            in_specs=[pl.BlockSpec((tm, k), lambda i, j, s: (i, 0)),
                      pl.BlockSpec((k, tn), lambda i, j, s: (0, j)),
                      *_two_source_specs((tm, tn), n_a, n_inner=1),
                      pl.BlockSpec((1, 1, tn), lambda i, j, s: (s[i], 0, j))],
            out_specs=pl.BlockSpec((tm, tn), lambda i, j, s: (i, j))),
        compiler_params=_cparams(("parallel", "arbitrary")),
        name="outproj",
    )(seq_of_tile, m, w, xa, xb, gate)


def _lagmat_kernel(c_ref, b_ref, o_ref):
    for g in range(c_ref.shape[0]):
        o_ref[g] = jnp.dot(c_ref[g], b_ref[g], preferred_element_type=F32,
                           precision=lax.Precision.HIGHEST)


def _lagmat(cs, bs):
    g2, c, n2 = cs.shape
    w = bs.shape[2]
    gb = _tile(g2, 8)
    return pl.pallas_call(
        _lagmat_kernel,
        out_shape=jax.ShapeDtypeStruct((g2, c, w), F32),
        grid=(g2 // gb,),
        in_specs=[pl.BlockSpec((gb, c, n2), lambda i: (i, 0, 0)),
                  pl.BlockSpec((gb, n2, w), lambda i: (i, 0, 0))],
        out_specs=pl.BlockSpec((gb, c, w), lambda i: (i, 0, 0)),
        compiler_params=_cparams(("arbitrary",)),
        name="ssm_lagmat",
    )(cs, bs)


def _ssm_tables(a_re, a_im, log_dt, b_re, b_im, c_re, c_im, d_skip):
    tc, ch = SSM_CHUNK, SSM_GROUP_CH
    _, g, n = a_re.shape
    dt = jnp.exp(log_dt)[:, :, None]
    zr, zi = a_re * dt, a_im * dt
    ks = jnp.arange(tc + 1, dtype=F32)[:, None, None, None]
    mag = jnp.exp(ks * zr[None])
    pw_re, pw_im = mag * jnp.cos(ks * zi[None]), mag * jnp.sin(ks * zi[None])
    lb_re, lb_im = pw_re[1], pw_im[1]
    den = a_re * a_re + a_im * a_im
    q_re = ((lb_re - 1.0) * a_re + lb_im * a_im) / den
    q_im = (lb_im * a_re - (lb_re - 1.0) * a_im) / den
    bb_re = q_re[..., None] * b_re - q_im[..., None] * b_im
    bb_im = q_re[..., None] * b_im + q_im[..., None] * b_re
    p_re, p_im = pw_re[:tc, ..., None], pw_im[:tc, ..., None]
    bk_re = p_re * bb_re[None] - p_im * bb_im[None]
    bk_im = p_re * bb_im[None] + p_im * bb_re[None]

    bs = jnp.concatenate([bk_re, bk_im], axis=3)
    bs = bs.transpose(1, 2, 3, 0, 4).reshape(2 * g, 2 * n, tc * ch)
    cs = jnp.concatenate([c_re, -c_im], axis=2)
    cs = jnp.broadcast_to(cs[None], (2, g, ch, 2 * n)).reshape(2 * g, ch, 2 * n)
    m = _lagmat(cs, bs).reshape(2, g, ch, tc, ch)

    s_idx = np.arange(tc)[:, None]
    t_idx = np.arange(tc)[None, :]
    lag_f = np.clip(t_idx - s_idx, 0, tc - 1)
    lag_b = np.clip(s_idx - t_idx, 0, tc - 1)
    mf = m[0][:, :, lag_f, :] * jnp.asarray(t_idx >= s_idx, F32)[None, None, :, :, None]
    mb = m[1][:, :, lag_b, :] * jnp.asarray(s_idx >= t_idx, F32)[None, None, :, :, None]
    toep = (mf + mb).transpose(0, 2, 4, 3, 1).reshape(g, tc * ch, tc * ch)

    def rows_sc(x):
        return x.transpose(1, 0, 3, 2).reshape(g, tc * ch, n)
    pin = jnp.concatenate([rows_sc(bk_re[::-1, 0]), rows_sc(bk_re[:, 1]),
                           rows_sc(bk_im[::-1, 0]), rows_sc(bk_im[:, 1])], axis=2)

    def w_tab(pr, pi):
        wr = c_re[None] * pr[:, :, None, :] - c_im[None] * pi[:, :, None, :]
        wi = c_re[None] * pi[:, :, None, :] + c_im[None] * pr[:, :, None, :]
        to_rows = lambda x: x.transpose(1, 3, 0, 2).reshape(g, n, tc * ch)
        return to_rows(wr), to_rows(-wi)
    qf_re, qf_im = w_tab(pw_re[1:tc + 1, 0], pw_im[1:tc + 1, 0])
    qb_re, qb_im = w_tab(pw_re[tc:0:-1, 1], pw_im[tc:0:-1, 1])
    qout = jnp.concatenate([qf_re, qb_re, qf_im, qb_im], axis=1)

    a_step_re = jnp.concatenate([pw_re[tc, 0], pw_re[tc, 1]], axis=1)
    a_step_im = jnp.concatenate([pw_im[tc, 0], pw_im[tc, 1]], axis=1)
    d_tile = jnp.tile(d_skip.reshape(g, 1, ch), (1, tc, 1)).reshape(g, tc * ch)
    return (toep.astype(BF16), pin.astype(BF16), qout.astype(BF16),
            a_step_re, a_step_im, d_tile)


def _gelu_tanh(x):
    return 0.5 * x * (1.0 + jnp.tanh(math.sqrt(2.0 / math.pi) * (x + 0.044715 * (x * x * x))))


def _ssm_kernel(u_ref, t_ref, p_ref, q_ref, are_ref, aim_ref, d_ref, o_ref,
                ug_ref, yg_ref, sin_re, sin_im, hf_re, hf_im, hb_re, hb_im):
    gb, w, _ = t_ref.shape
    tc, ch = SSM_CHUNK, SSM_GROUP_CH
    nc = u_ref.shape[0] // tc
    half = w // 2
    n_col = w // LANES
    slots = LANES // ch
    slot = lax.broadcasted_iota(I32, (8, LANES), 1) // ch
    in_slot = [slot == k for k in range(slots)]

    def to_chunks(rc, _):
        xs = [u_ref[pl.ds(rc * (8 * tc) + t, 8, stride=tc), :] for t in range(tc)]
        cols = [[jnp.zeros((8, LANES), F32) for _ in range(n_col)] for _ in range(gb)]
        for col in range(n_col):
            for s in range(slots):
                v = jnp.zeros((8, LANES), F32)
                for ts in range(slots):
                    v = jnp.where(in_slot[(ts - s) % slots], xs[col * slots + ts], v)
                vr = v if s == 0 else pltpu.roll(v, s * ch, 1)
                for ts in range(slots):
                    g = (ts - s) % slots
                    cols[g][col] = jnp.where(in_slot[ts], vr, cols[g][col])
        for g in range(gb):
            ug_ref[g, pl.ds(rc * 8, 8), :] = jnp.concatenate(cols[g], axis=1)
        return 0

    lax.fori_loop(0, nc // 8, to_chunks, 0)

    for g in range(gb):
        s_in = _bdot(ug_ref[g].astype(BF16), p_ref[g])
        sin_re[pl.ds(g, nc, stride=gb), :] = s_in[:, :half]
        sin_im[pl.ds(g, nc, stride=gb), :] = s_in[:, half:]

    a_re = are_ref[...]
    a_im = aim_ref[...]
    is_fwd = lax.broadcasted_iota(I32, (gb, half), 1) < (half // 2)

    def step(j, carry):
        s_re, s_im = carry
        at_f = pl.ds(pl.multiple_of(j * gb, gb), gb)
        at_b = pl.ds(pl.multiple_of((nc - 1 - j) * gb, gb), gb)
        hf_re[at_f, :] = s_re
        hf_im[at_f, :] = s_im
        hb_re[at_b, :] = s_re
        hb_im[at_b, :] = s_im
        in_re = jnp.where(is_fwd, sin_re[at_f, :], sin_re[at_b, :])
        in_im = jnp.where(is_fwd, sin_im[at_f, :], sin_im[at_b, :])
        return (a_re * s_re - a_im * s_im + in_re, a_re * s_im + a_im * s_re + in_im)

    zero = jnp.zeros((gb, half), F32)
    lax.fori_loop(0, nc, step, (zero, zero))

    fwd_lane = lax.broadcasted_iota(I32, (nc, half), 1) < (half // 2)
    for g in range(gb):
        u = ug_ref[g]
        rows = pl.ds(g, nc, stride=gb)
        hin = jnp.concatenate([jnp.where(fwd_lane, hf_re[rows, :], hb_re[rows, :]),
                               jnp.where(fwd_lane, hf_im[rows, :], hb_im[rows, :])], axis=1)
        y = _bdot(u.astype(BF16), t_ref[g]) + _bdot(hin.astype(BF16), q_ref[g]) + d_ref[pl.ds(g, 1), :] * u
        yg_ref[g] = _gelu_tanh(y)

    def from_chunks(rc, _):
        for col in range(n_col):
            ys = [yg_ref[g, pl.ds(rc * 8, 8), pl.ds(col * LANES, LANES)] for g in range(gb)]
            z = [jnp.zeros((8, LANES), F32) for _ in range(slots)]
            for s in range(slots):
                v = jnp.zeros((8, LANES), F32)
                for g in range(gb):
                    v = jnp.where(in_slot[(g - s) % slots], ys[g], v)
                vr = v if s == 0 else pltpu.roll(v, s * ch, 1)
                for g in range(gb):
                    ts = (g - s) % slots
                    z[ts] = jnp.where(in_slot[g], vr, z[ts])
            for ts in range(slots):
                o_ref[pl.ds(rc * (8 * tc) + col * slots + ts, 8, stride=tc), :] = z[ts]
        return 0

    lax.fori_loop(0, nc // 8, from_chunks, 0)


def _ssm_seqs(u, tabs, row0, n_seq, seq_len):
    toep, pin, qout, a_re, a_im, d_tile = tabs
    _, wch = u.shape
    w = toep.shape[1]
    half = w // 2
    gb = LANES // SSM_GROUP_CH
    nc = seq_len // SSM_CHUNK
    assert nc % 8 == 0 and row0 % seq_len == 0 and wch % LANES == 0
    blk0 = row0 // seq_len
    tab_spec = pl.BlockSpec((gb, w, w), lambda i, b: (i, 0, 0))
    vec_spec = lambda width: pl.BlockSpec((gb, width), lambda i, b: (i, 0))
    return pl.pallas_call(
        _ssm_kernel,
        out_shape=jax.ShapeDtypeStruct((n_seq * seq_len, wch), F32),
        grid=(wch // LANES, n_seq),
        in_specs=[pl.BlockSpec((seq_len, LANES), lambda i, b: (blk0 + b, i)),
                  tab_spec, tab_spec, tab_spec,
                  vec_spec(half), vec_spec(half), vec_spec(w)],
        out_specs=pl.BlockSpec((seq_len, LANES), lambda i, b: (b, i)),
        scratch_shapes=[pltpu.VMEM((gb, nc, w), F32)] * 2 + [pltpu.VMEM((gb * nc, half), F32)] * 6,
        compiler_params=_cparams(("parallel", "arbitrary")),
        name="ssm",
    )(u, toep, pin, qout, a_re, a_im, d_tile)


def _ssm_branch(u, tabs, seqs):
    outs = []
    row0 = 0
    for n_seq, seq_len in seqs:
        outs.append(_ssm_seqs(u, tabs, row0, n_seq, seq_len))
        row0 += n_seq * seq_len
    return outs


def _conv_kernel(first_ref, last_ref, prev_ref, cur_ref, next_ref, w_ref, b_ref, g_ref, beta_ref,
                 o_ref, xs_ref, acc_ref, *, halo, row_chunk):
    i = pl.program_id(0)
    r, c = cur_ref.shape
    kw = w_ref.shape[0]
    half = kw // 2
    n_shift = xs_ref.shape[0]
    keep_prev = jnp.where(first_ref[i] == 0, 1.0, 0.0)
    keep_next = jnp.where(last_ref[i] == 0, 1.0, 0.0)
    xs_ref[0, pl.ds(0, halo), :] = prev_ref[...].astype(F32) * keep_prev
    xs_ref[0, pl.ds(halo, r), :] = cur_ref[...].astype(F32)
    xs_ref[0, pl.ds(halo + r, halo), :] = next_ref[...].astype(F32) * keep_next
    span = r + 2 * halo - n_shift
    for b in range(1, n_shift):
        xs_ref[b, pl.ds(0, span), :] = xs_ref[0, pl.ds(b, span), :]

    col_chunk = _tile(c, 4 * LANES)
    n_cc = c // col_chunk

    def chunk(it, _):
        r0 = pl.multiple_of((it // n_cc) * row_chunk, row_chunk)
        c0 = pl.multiple_of((it % n_cc) * col_chunk, col_chunk)
        cols = pl.ds(c0, col_chunk)
        acc = jnp.zeros((row_chunk, col_chunk), F32)
        for k in range(kw):
            off = halo - half + k
            b = off % n_shift
            acc = acc + w_ref[pl.ds(k, 1), cols] * xs_ref[b, pl.ds(r0 + (off - b), row_chunk), cols]
        acc_ref[pl.ds(r0, row_chunk), cols] = acc
        return 0

    lax.fori_loop(0, (r // row_chunk) * n_cc, chunk, 0)
    ln_rows = _tile(r, 16)

    def norm_rows(it, _):
        rows = pl.ds(pl.multiple_of(it * ln_rows, ln_rows), ln_rows)
        y = acc_ref[rows, :] + b_ref[...]
        mu = jnp.mean(y, axis=-1, keepdims=True)
        yc = y - mu
        var = jnp.mean(yc * yc, axis=-1, keepdims=True)
        z = yc * lax.rsqrt(var + LN_EPS) * g_ref[...] + beta_ref[...]
        o_ref[rows, :] = (z * _sigmoid(z)).astype(o_ref.dtype)
        return 0

    lax.fori_loop(0, r // ln_rows, norm_rows, 0, unroll=2)


def _conv_ln(ub, conv_w, conv_b, ln_g, ln_b, first_of_tile, last_of_tile, r):
    t, c = ub.shape
    kw = conv_w.shape[0]
    halo = 16
    assert kw // 2 <= halo and r % halo == 0
    nb = r // halo
    n_halo_blocks = t // halo
    row_chunk = _tile(r, 32)
    vec = lambda i, f, l: (0, 0)
    return pl.pallas_call(
        functools.partial(_conv_kernel, halo=halo, row_chunk=row_chunk),
        out_shape=jax.ShapeDtypeStruct((t, c), BF16),
        grid_spec=pltpu.PrefetchScalarGridSpec(
            num_scalar_prefetch=2, grid=(t // r,),
            in_specs=[pl.BlockSpec((halo, c), lambda i, f, l: (jnp.maximum(i * nb - 1, 0), 0)),
                      pl.BlockSpec((r, c), lambda i, f, l: (i, 0)),
                      pl.BlockSpec((halo, c), lambda i, f, l: (jnp.minimum((i + 1) * nb, n_halo_blocks - 1), 0)),
                      pl.BlockSpec((kw, c), vec), pl.BlockSpec((1, c), vec),
                      pl.BlockSpec((1, c), vec), pl.BlockSpec((1, c), vec)],
            out_specs=pl.BlockSpec((r, c), lambda i, f, l: (i, 0)),
            scratch_shapes=[pltpu.VMEM((8, r + 2 * halo, c), F32), pltpu.VMEM((r, c), F32)]),
        compiler_params=_cparams(("arbitrary",)),
        name="conv_ln",
    )(first_of_tile, last_of_tile, ub, ub, ub, conv_w, conv_b.reshape(1, c),
      ln_g.reshape(1, c), ln_b.reshape(1, c))


def _seg_allreduce(x, lane, seg, op):
    n = x.shape[-1]
    s = 1
    while s < seg:
        up = pltpu.roll(x, n - s, 1)
        dn = pltpu.roll(x, s, 1)
        x = op(x, jnp.where((lane & s) == 0, up, dn))
        s *= 2
    return x


def _route_kernel(lg_ref, bias_ref, idx_ref, wt_ref, rank_ref, cnt_ref, carry_ref, *, n_groups, topk_groups, top_k):
    i = pl.program_id(0)
    tm, e = lg_ref.shape
    seg = e // n_groups
    lane = lax.broadcasted_iota(I32, (tm, e), 1)
    lane_f = lane.astype(F32)
    neg = jnp.float32(-jnp.inf)

    scores = _sigmoid(lg_ref[...])
    choice = scores + bias_ref[...]

    m1 = _seg_allreduce(choice, lane, seg, jnp.maximum)
    first = _seg_allreduce(jnp.where(choice == m1, lane_f, float(e)), lane, seg, jnp.minimum)
    m2 = _seg_allreduce(jnp.where(lane_f == first, neg, choice), lane, seg, jnp.maximum)
    gscore = m1 + m2

    beaten = jnp.zeros((tm, e), F32)
    for dshift in range(1, n_groups):
        other = pltpu.roll(gscore, dshift * seg, 1)
        other_is_lower = lane >= dshift * seg
        beaten = beaten + jnp.where(other_is_lower, jnp.where(other >= gscore, 1.0, 0.0),
                                    jnp.where(other > gscore, 1.0, 0.0))
    masked = jnp.where(beaten < topk_groups, choice, neg)

    idx_out = jnp.zeros((tm, e), F32)
    sc_out = jnp.zeros((tm, e), F32)
    sel = jnp.zeros((tm, e), F32)
    picks = []
    for k in range(top_k):
        m = jnp.max(masked, axis=1, keepdims=True)
        pick = jnp.min(jnp.where(masked == m, lane_f, float(e)), axis=1, keepdims=True)
        hit = lane_f == pick
        s_k = jnp.sum(jnp.where(hit, scores, 0.0), axis=1, keepdims=True)
        idx_out = jnp.where(lane == k, pick, idx_out)
        sc_out = jnp.where(lane == k, s_k, sc_out)
        sel = jnp.where(hit, 1.0, sel)
        masked = jnp.where(hit, neg, masked)
        picks.append(hit)
    denom = jnp.sum(sc_out, axis=1, keepdims=True)
    idx_ref[...] = idx_out.astype(I32)
    wt_ref[...] = sc_out / denom * ROUTED_SCALE

    @pl.when(i == 0)
    def _():
        carry_ref[...] = jnp.zeros_like(carry_ref)

    rows = lax.broadcasted_iota(I32, (tm, tm), 0)
    cols = lax.broadcasted_iota(I32, (tm, tm), 1)
    tri = jnp.where(cols < rows, 1.0, 0.0).astype(BF16)
    rank = _bdot(tri, sel.astype(BF16)) + carry_ref[...]
    rank_out = jnp.zeros((tm, e), F32)
    for k in range(top_k):
        r_k = jnp.sum(jnp.where(picks[k], rank, 0.0), axis=1, keepdims=True)
        rank_out = jnp.where(lane == k, r_k, rank_out)
    rank_ref[...] = rank_out.astype(I32)
    carry_ref[...] = carry_ref[...] + jnp.sum(sel, axis=0, keepdims=True)
    cnt_ref[...] = carry_ref[...].astype(I32)


def _route(logits, bias, tm):
    t, e = logits.shape
    row = lambda i: (i, 0)
    fixed = lambda i: (0, 0)
    return pl.pallas_call(
        functools.partial(_route_kernel, n_groups=N_ROUTE_GROUPS, topk_groups=TOPK_ROUTE_GROUPS, top_k=TOP_K),
        out_shape=(jax.ShapeDtypeStruct((t, e), I32), jax.ShapeDtypeStruct((t, e), F32),
                   jax.ShapeDtypeStruct((t, e), I32), jax.ShapeDtypeStruct((1, e), I32)),
        grid=(t // tm,),
        in_specs=[pl.BlockSpec((tm, e), row), pl.BlockSpec((1, e), fixed)],
        out_specs=(pl.BlockSpec((tm, e), row), pl.BlockSpec((tm, e), row),
                   pl.BlockSpec((tm, e), row), pl.BlockSpec((1, e), fixed)),
        scratch_shapes=[pltpu.VMEM((1, e), F32)],
        compiler_params=_cparams(("arbitrary",)),
        name="route",
    )(logits, bias.reshape(1, e))


def _dest_kernel(idx_ref, rank_ref, start_ref, o_ref, *, top_k):
    tm, e = idx_ref.shape
    lane = lax.broadcasted_iota(I32, (tm, e), 1)
    idx = idx_ref[...]
    start = start_ref[...].astype(F32)
    out = jnp.zeros((tm, e), F32)
    for k in range(top_k):
        s_k = jnp.sum(jnp.where(lane == idx[:, k:k + 1], start, 0.0), axis=1, keepdims=True)
        out = jnp.where(lane == k, s_k, out)
    o_ref[...] = out.astype(I32) + rank_ref[...]


def _dest_rows(top_idx, rank, pad_start, tm):
    t, e = top_idx.shape
    row = lambda i: (i, 0)
    return pl.pallas_call(
        functools.partial(_dest_kernel, top_k=TOP_K),
        out_shape=jax.ShapeDtypeStruct((t, e), I32),
        grid=(t // tm,),
        in_specs=[pl.BlockSpec((tm, e), row), pl.BlockSpec((tm, e), row), pl.BlockSpec((1, e), lambda i: (0, 0))],
        out_specs=pl.BlockSpec((tm, e), row),
        compiler_params=_cparams(("arbitrary",)),
        name="dest_rows",
    )(top_idx, rank, pad_start.reshape(1, e))


def _expert_kernel(blk_e_ref, nblk_ref, idx_hbm, x_hbm, wg_ref, wu_ref, wd_ref, ys_hbm,
                   idx_smem, xb0, xb1, ob0, ob1, sem_idx, sem_g, sem_s):
    del blk_e_ref
    b = pl.program_id(0)
    n = nblk_ref[0]
    kh = wg_ref.shape[1] // 2
    ns = kh // LANES
    pitch = ns + 1
    r = xb0.shape[0] // pitch
    xbufs = (xb0, xb1)
    obufs = (ob0, ob1)
    dump0 = ys_hbm.shape[0] // pitch - 2 * r

    def idx_copy(blk, s):
        return pltpu.make_async_copy(idx_hbm.at[blk], idx_smem.at[s], sem_idx.at[s])

    def x_row(row):
        return x_hbm.at[pl.ds(pl.multiple_of(row * ns, ns), ns)]

    def y_row(row):
        return ys_hbm.at[pl.ds(row * pitch, pitch)]

    def gather_start(s):
        for i in range(r):
            pltpu.make_async_copy(x_row(idx_smem[s, 0, i]), xbufs[s].at[pl.ds(i * pitch, ns)], sem_g.at[s]).start()

    def gather_wait(s):
        for i in range(r):
            pltpu.make_async_copy(x_row(0), xbufs[s].at[pl.ds(i * pitch, ns)], sem_g.at[s]).wait()

    def scatter_start(s):
        for i in range(r):
            pltpu.make_async_copy(obufs[s].at[pl.ds(i * pitch, pitch)], y_row(idx_smem[s, 1, i]), sem_s.at[s]).start()

    def scatter_wait(s):
        for i in range(r):
            pltpu.make_async_copy(obufs[s].at[pl.ds(i * pitch, pitch)], y_row(0), sem_s.at[s]).wait()

    @pl.when(b == 0)
    def _():
        ob0[...] = jnp.zeros_like(ob0)
        ob1[...] = jnp.zeros_like(ob1)
        for h in range(2):
            fill = pltpu.make_async_copy(ob1, ys_hbm.at[pl.ds((dump0 + h * r) * pitch, r * pitch)], sem_s.at[1])
            fill.start()
            fill.wait()
        first = idx_copy(0, 0)
        first.start()
        first.wait()
        gather_start(0)

        @pl.when(n > 1)
        def _():
            idx_copy(1, 1).start()

    def step(cur):
        nxt = 1 - cur
        gather_wait(cur)

        @pl.when(b + 1 < n)
        def _():
            idx_copy(b + 1, nxt).wait()
            gather_start(nxt)

        @pl.when(b >= 2)
        def _():
            scatter_wait(cur)

        lo, hi = _unpack_bf16_pair(_load_row_slabs(xbufs[cur], r, ns, pitch))
        lo = lo.astype(BF16)
        hi = hi.astype(BF16)
        gate = _bdot(lo, wg_ref[0, pl.ds(0, kh), :]) + _bdot(hi, wg_ref[0, pl.ds(kh, kh), :])
        up = _bdot(lo, wu_ref[0, pl.ds(0, kh), :]) + _bdot(hi, wu_ref[0, pl.ds(kh, kh), :])
        act = (gate * _sigmoid(gate) * up).astype(BF16)
        out = _bdot(act, wd_ref[0])
        _store_row_slabs(obufs[cur], _pack_bf16_pair(out[:, :kh], out[:, kh:]), pitch)
        scatter_start(cur)

        @pl.when(b + 2 < n)
        def _():
            idx_copy(b + 2, cur).start()

        @pl.when(b == n - 1)
        def _():
            scatter_wait(cur)

            @pl.when(b >= 1)
            def _():
                scatter_wait(nxt)

    for parity in range(2):
        @pl.when((b < n) & ((b & 1) == parity))
        def _(parity=parity):
            step(parity)


def _experts(blk_e, nblk, row_idx, xp, wg, wu, wd, n_out_rows):
    n_blocks, _, r = row_idx.shape
    n_e, d, de = wg.shape
    ns = d // 2 // LANES
    last = lambda b, be, nb: jnp.minimum(b, nb[0] - 1)
    return pl.pallas_call(
        _expert_kernel,
        out_shape=jax.ShapeDtypeStruct(((n_out_rows + 2 * r) * (ns + 1), LANES), U32),
        grid_spec=pltpu.PrefetchScalarGridSpec(
            num_scalar_prefetch=2, grid=(n_blocks,),
            in_specs=[pl.BlockSpec(memory_space=pl.ANY),
                      pl.BlockSpec(memory_space=pl.ANY),
                      pl.BlockSpec((1, d, de), lambda b, be, nb: (be[last(b, be, nb)], 0, 0)),
                      pl.BlockSpec((1, d, de), lambda b, be, nb: (be[last(b, be, nb)], 0, 0)),
                      pl.BlockSpec((1, de, d), lambda b, be, nb: (be[last(b, be, nb)], 0, 0))],
            out_specs=pl.BlockSpec(memory_space=pl.ANY),
            scratch_shapes=[pltpu.SMEM((2, 2, r), I32)] + [pltpu.VMEM((r * (ns + 1), LANES), U32)] * 4
                           + [pltpu.SemaphoreType.DMA((2,))] * 3),
        compiler_params=_cparams(("arbitrary",)),
        name="experts",
    )(blk_e, nblk, row_idx, xp, wg, wu, wd)


def _combine_kernel(seq_ref, *refs, top_k, n_a):
    del seq_ref
    y_refs = refs[:top_k]
    wt_ref, act_ref, wsd_ref, x_ref, g2_ref, fn_ref, oa_ref, ob_ref, routed_ref = refs[top_k:]
    i = pl.program_id(0)
    tm, d = x_ref.shape
    kh = d // 2
    ns = kh // LANES
    wt = wt_ref[...]
    w_cols = [wt[:, k:k + 1] for k in range(top_k)]
    for j in range(ns):
        r_lo = jnp.zeros((tm, LANES), F32)
        r_hi = jnp.zeros((tm, LANES), F32)
        for k in range(top_k):
            lo, hi = _unpack_bf16_pair(y_refs[k][pl.ds(j, tm, stride=ns + 1), :])
            r_lo = r_lo + w_cols[k] * lo
            r_hi = r_hi + w_cols[k] * hi
        routed_ref[:, pl.ds(j * LANES, LANES)] = r_lo
        routed_ref[:, pl.ds(kh + j * LANES, LANES)] = r_hi
    shared = _bdot(act_ref[...], wsd_ref[...])
    y = x_ref[...] + g2_ref[0] * (shared + routed_ref[...])
    ms = jnp.mean(y * y, axis=-1, keepdims=True)
    out = y * lax.rsqrt(ms + RMS_EPS) * fn_ref[...]

    @pl.when(i < n_a)
    def _():
        oa_ref[...] = out

    @pl.when(i >= n_a)
    def _():
        ob_ref[...] = out


def _combine(ys, wt, act, wsd, x1, g2, final_norm, seq_of_tile, tm, rows_a):
    t, d = x1.shape
    n_a = rows_a // tm
    e = wt.shape[1]
    ds_ = act.shape[1]
    ns = d // 2 // LANES
    nt = t // tm
    row = lambda i, s: (i, 0)
    fixed = lambda i, s: (0, 0)
    y_specs = [pl.BlockSpec((tm * (ns + 1), LANES), functools.partial(lambda i, s, k: (k * nt + i, 0), k=k))
               for k in range(TOP_K)]
    return pl.pallas_call(
        functools.partial(_combine_kernel, top_k=TOP_K, n_a=n_a),
        out_shape=(jax.ShapeDtypeStruct((rows_a, d), F32), jax.ShapeDtypeStruct((t - rows_a, d), F32)),
        grid_spec=pltpu.PrefetchScalarGridSpec(
            num_scalar_prefetch=1, grid=(nt,),
            in_specs=y_specs + [pl.BlockSpec((tm, e), row),
                                pl.BlockSpec((tm, ds_), row),
                                pl.BlockSpec((ds_, d), fixed),
                                pl.BlockSpec((tm, d), row),
                                pl.BlockSpec((1, 1, d), lambda i, s: (s[i], 0, 0)),
                                pl.BlockSpec((1, d), fixed)],
            out_specs=_two_source_specs((tm, d), n_a),
            scratch_shapes=[pltpu.VMEM((tm, d), F32)]),
        compiler_params=_cparams(("arbitrary",)),
        name="combine",
    )(seq_of_tile, *([ys] * TOP_K), wt, act, wsd, x1, g2, final_norm.reshape(1, d))


def _seq_tables(seqs, tile):
    sid, first, last = [], [], []
    s = 0
    for n_seq, seq_len in seqs:
        per = seq_len // tile
        for _ in range(n_seq):
            sid += [s] * per
            first += [1] + [0] * (per - 1)
            last += [0] * (per - 1) + [1]
            s += 1
    as_i32 = lambda v: jnp.asarray(np.asarray(v, np.int32))
    return as_i32(sid), as_i32(first), as_i32(last)


def _layer(xa, xb, c_all, seqs, p):
    d = xa.shape[1]
    t = xa.shape[0] + xb.shape[0]
    w_ssm = p["ssm_d"].shape[0]
    w_conv = p["conv_b"].shape[0]
    n_e = p["router_w"].shape[1]
    min_seq = min(sl for _, sl in seqs)

    tm = _tile(min_seq, 1024)
    tn_row = _tile(min_seq, 256)
    seq_mm, _, _ = _seq_tables(seqs, tm)
    seq_nr, _, _ = _seq_tables(seqs, tn_row)

    n_seq_total = c_all.shape[0]
    ada = _ada(c_all, p["w_ada"], p["b_ada"])
    sh1, sc1, g1, sh2, sc2, g2 = [a.reshape(n_seq_total, 1, d) for a in jnp.split(ada, 6, axis=1)]

    bf = lambda w: w.astype(BF16)
    w_in = p["w_in"]
    h = _norm1(xa, xb, p["norm_mix"], sc1, sh1, seq_nr, tn_row)

    u_ssm = _proj(h, bf(w_in[:, :w_ssm]), tm, _tile(w_ssm, 1024), out_dtype=F32, name="in_ssm")
    tabs = _ssm_tables(p["ssm_a_re"], p["ssm_a_im"], p["ssm_log_dt"], p["ssm_b_re"], p["ssm_b_im"],
                       p["ssm_c_re"], p["ssm_c_im"], p["ssm_d"])
    ya_a, ya_b = _ssm_branch(u_ssm, tabs, seqs)
    wgl = p["w_ssm_glu"]
    a_act = _glu_proj(ya_a, bf(wgl[:, :w_ssm]), bf(wgl[:, w_ssm:]), _tile(min_seq, 512), _tile(w_ssm, 1024), False,
                      "ssm_glu", x_tail=ya_b)

    ub = _glu_proj(h, bf(w_in[:, w_ssm:w_ssm + w_conv]), bf(w_in[:, w_ssm + w_conv:]),
                   tm, _tile(w_conv, 512), False, "in_conv_glu")
    r_conv = _tile(min_seq, 256)
    _, first_c, last_c = _seq_tables(seqs, r_conv)
    cv = _conv_ln(ub, p["conv_w"], p["conv_b"], p["conv_norm_g"], p["conv_norm_b"], first_c, last_c, r_conv)

    wmg = p["w_merge_gate"]
    bmg = p["b_merge_gate"]
    tm_merge = _tile(min_seq, 512)
    m = _merge(h, a_act, cv, bf(wmg[:, :d]), bf(wmg[:, d:]), bf(p["w_branch_ssm"]), bf(p["w_branch_conv"]),
               bmg[:d].reshape(1, d), bmg[d:].reshape(1, d), tm_merge, _tile(d, 512))
    x1 = _outproj(m, bf(p["w_out"]), xa, xb, g1, seq_mm, tm, _tile(d, 512))

    h2, h2p, logits = _norm2(x1, p["norm_ffn"], sc2, sh2, p["router_w"], seq_nr, tn_row)
    tm_route = _tile(t, 256)
    top_idx, top_w, rank, counts = _route(logits, p["router_bias"], tm_route)

    r_blk = 256 if t * TOP_K >= 256 * n_e else 8
    counts = counts[0]
    padded = (counts + r_blk - 1) // r_blk * r_blk
    pad_end = jnp.cumsum(padded)
    pad_start = pad_end - padded
    n_blocks = -(-(t * TOP_K) // r_blk) + n_e
    dest = _dest_rows(top_idx, rank, pad_start, tm_route)[:, :TOP_K]
    pos = jnp.arange(n_blocks * r_blk, dtype=I32)
    dump = TOP_K * t + ((pos // r_blk) % 2) * r_blk + pos % r_blk
    out_rows = jnp.arange(TOP_K, dtype=I32)[None, :] * t + jnp.arange(t, dtype=I32)[:, None]
    row_dst = dump.at[dest.reshape(-1)].set(out_rows.reshape(-1))
    row_src = jnp.where(row_dst < TOP_K * t, row_dst % t, 0)
    row_idx = jnp.stack([row_src.reshape(n_blocks, r_blk), row_dst.reshape(n_blocks, r_blk)], axis=1)
    blk_start = jnp.arange(n_blocks, dtype=I32) * r_blk
    blk_e = jnp.minimum(jnp.searchsorted(pad_end, blk_start, side="right"), n_e - 1).astype(I32)
    nblk = (pad_end[-1] // r_blk).astype(I32).reshape(1)

    ys = _experts(blk_e, nblk, row_idx, h2p, bf(p["exp_w_gate"]), bf(p["exp_w_up"]), bf(p["exp_w_down"]),
                  TOP_K * t)

    act = _glu_proj(h2, bf(p["shared_w_gate"]), bf(p["shared_w_up"]), tm, p["shared_w_gate"].shape[1],
                    True, "shared_act")
    tm_c = _tile(min_seq, 128)
    seq_c, _, _ = _seq_tables(seqs, tm_c)
    return _combine(ys, top_w, act, bf(p["shared_w_down"]), x1, g2, p["final_norm"], seq_c, tm_c, xa.shape[0])


def kernel(x_prompt, x_sample, c_prompt, c_sample, w_ada, b_ada, norm_mix, w_in, ssm_a_re, ssm_a_im, ssm_log_dt, ssm_b_re, ssm_b_im, ssm_c_re, ssm_c_im, ssm_d, w_ssm_glu, w_branch_ssm, conv_w, conv_b, conv_norm_g, conv_norm_b, w_branch_conv, w_merge_gate, b_merge_gate, w_out, norm_ffn, router_w, router_bias, exp_w_gate, exp_w_up, exp_w_down, shared_w_gate, shared_w_up, shared_w_down, final_norm):
    assert w_ada.shape[0] == 1, "single-layer trunk"
    bp, lp, d = x_prompt.shape
    bs, ls, _ = x_sample.shape
    seqs = [(bp, lp), (bs, ls)]
    n_seq = bp + bs
    c_all = jnp.concatenate([c_prompt, c_sample, jnp.zeros((-n_seq % 8, d), F32)], axis=0)
    params = dict(
        w_ada=w_ada[0], b_ada=b_ada[0], norm_mix=norm_mix[0], w_in=w_in[0],
        ssm_a_re=ssm_a_re[0], ssm_a_im=ssm_a_im[0], ssm_log_dt=ssm_log_dt[0],
        ssm_b_re=ssm_b_re[0], ssm_b_im=ssm_b_im[0], ssm_c_re=ssm_c_re[0], ssm_c_im=ssm_c_im[0],
        ssm_d=ssm_d[0], w_ssm_glu=w_ssm_glu[0], w_branch_ssm=w_branch_ssm[0],
        conv_w=conv_w[0], conv_b=conv_b[0], conv_norm_g=conv_norm_g[0], conv_norm_b=conv_norm_b[0],
        w_branch_conv=w_branch_conv[0], w_merge_gate=w_merge_gate[0], b_merge_gate=b_merge_gate[0],
        w_out=w_out[0], norm_ffn=norm_ffn[0], router_w=router_w[0], router_bias=router_bias[0],
        exp_w_gate=exp_w_gate[0], exp_w_up=exp_w_up[0], exp_w_down=exp_w_down[0],
        shared_w_gate=shared_w_gate[0], shared_w_up=shared_w_up[0], shared_w_down=shared_w_down[0],
        final_norm=final_norm)
    y_a, y_b = _layer(x_prompt.reshape(bp * lp, d), x_sample.reshape(bs * ls, d), c_all, seqs, params)
    return (y_a.reshape(bp, lp, d), y_b.reshape(bs, ls, d))
```

```python
import functools
import math

import jax
import jax.numpy as jnp
import numpy as np
from jax import lax
from jax.experimental import pallas as pl
from jax.experimental.pallas import tpu as pltpu

F32 = jnp.float32
BF16 = jnp.bfloat16
I32 = jnp.int32
U32 = jnp.uint32

SSM_GROUP_CH = 16
SSM_CHUNK = 16
N_ROUTE_GROUPS = 8
TOPK_ROUTE_GROUPS = 4
TOP_K = 8
ROUTED_SCALE = 2.5
RMS_EPS = 1e-6
LN_EPS = 1e-5

V7X_VMEM_LIMIT = 56 * 1024 * 1024
LANES = 128


def _cparams(sem, vmem=V7X_VMEM_LIMIT):
    return pltpu.CompilerParams(dimension_semantics=sem, vmem_limit_bytes=vmem)


def _tile(n, want):
    t = min(n, want)
    while n % t:
        t -= 1
    return t


def _sigmoid(x):
    return 1.0 / (1.0 + jnp.exp(-x))


def _bdot(a, b):
    return jnp.dot(a, b, preferred_element_type=F32)


def _ada_kernel(c_ref, w_ref, b_ref, o_ref):
    c = c_ref[...]
    s = (c * _sigmoid(c)).astype(BF16)
    o_ref[...] = _bdot(s, w_ref[...].astype(BF16)) + b_ref[...]


def _ada(c_all, w_ada, b_ada):
    nseq, d = c_all.shape
    n = w_ada.shape[1]
    tn = _tile(n, 512)
    return pl.pallas_call(
        _ada_kernel,
        out_shape=jax.ShapeDtypeStruct((nseq, n), F32),
        grid=(n // tn,),
        in_specs=[pl.BlockSpec((nseq, d), lambda j: (0, 0)),
                  pl.BlockSpec((d, tn), lambda j: (0, j)),
                  pl.BlockSpec((1, tn), lambda j: (0, j))],
        out_specs=pl.BlockSpec((nseq, tn), lambda j: (0, j)),
        compiler_params=_cparams(("arbitrary",)),
        name="ada",
    )(c_all, w_ada, b_ada.reshape(1, n))


def _pack_bf16_pair(lo, hi):
    lo_b = pltpu.bitcast(lo.astype(BF16).astype(F32), U32) >> 16
    hi_b = pltpu.bitcast(hi.astype(BF16).astype(F32), U32) & jnp.uint32(0xFFFF0000)
    return hi_b | lo_b


def _unpack_bf16_pair(w):
    lo = pltpu.bitcast(w << 16, F32)
    hi = pltpu.bitcast(w & jnp.uint32(0xFFFF0000), F32)
    return lo, hi


def _store_row_slabs(ref, val, pitch):
    rows, width = val.shape
    for j in range(width // LANES):
        ref[pl.ds(j, rows, stride=pitch), :] = val[:, j * LANES:(j + 1) * LANES]


def _load_row_slabs(ref, rows, n_tiles, pitch):
    return jnp.concatenate([ref[pl.ds(j, rows, stride=pitch), :] for j in range(n_tiles)], axis=1)


def _modulated_norm(x, g, sc, sh):
    ms = jnp.mean(x * x, axis=-1, keepdims=True)
    return x * lax.rsqrt(ms + RMS_EPS) * g * (1.0 + sc) + sh


def _two_source_specs(block, n_a, n_inner=0):
    if n_inner:
        first = lambda i, j, *_: (jnp.minimum(i, n_a - 1), jnp.where(i < n_a, j, 0))
        second = lambda i, j, *_: (jnp.maximum(i - n_a, 0), jnp.where(i < n_a, 0, j))
    else:
        first = lambda i, *_: (jnp.minimum(i, n_a - 1), 0)
        second = lambda i, *_: (jnp.maximum(i - n_a, 0), 0)
    return pl.BlockSpec(block, first), pl.BlockSpec(block, second)


def _pick_source(i, n_a, a_ref, b_ref):
    return jnp.where(i < n_a, a_ref[...], b_ref[...])


def _norm1_kernel(seq_ref, xa_ref, xb_ref, g_ref, sc_ref, sh_ref, o_ref, *, n_a):
    del seq_ref
    x = _pick_source(pl.program_id(0), n_a, xa_ref, xb_ref)
    o_ref[...] = _modulated_norm(x, g_ref[...], sc_ref[0], sh_ref[0]).astype(BF16)


def _norm2_kernel(seq_ref, x_ref, g_ref, sc_ref, sh_ref, rw_ref, h_ref, hp_ref, lg_ref):
    del seq_ref
    h = _modulated_norm(x_ref[...], g_ref[...], sc_ref[0], sh_ref[0])
    half = h.shape[1] // 2
    h_ref[...] = h.astype(BF16)
    _store_row_slabs(hp_ref, _pack_bf16_pair(h[:, :half], h[:, half:]), half // LANES)
    lg_ref[...] = jnp.dot(h, rw_ref[...], preferred_element_type=F32,
                          precision=lax.Precision.HIGHEST)


def _norm1(xa, xb, g, sc, sh, seq_of_tile, tm):
    d = xa.shape[1]
    t = xa.shape[0] + xb.shape[0]
    n_a = xa.shape[0] // tm
    row = lambda i, s: (i, 0)
    per_seq = lambda i, s: (s[i], 0, 0)
    return pl.pallas_call(
        functools.partial(_norm1_kernel, n_a=n_a),
        out_shape=jax.ShapeDtypeStruct((t, d), BF16),
        grid_spec=pltpu.PrefetchScalarGridSpec(
            num_scalar_prefetch=1, grid=(t // tm,),
            in_specs=[*_two_source_specs((tm, d), n_a),
                      pl.BlockSpec((1, d), lambda i, s: (0, 0)),
                      pl.BlockSpec((1, 1, d), per_seq),
                      pl.BlockSpec((1, 1, d), per_seq)],
            out_specs=pl.BlockSpec((tm, d), row)),
        compiler_params=_cparams(("arbitrary",)),
        name="norm1",
    )(seq_of_tile, xa, xb, g.reshape(1, d), sc, sh)


def _norm2(x, g, sc, sh, router_w, seq_of_tile, tm):
    t, d = x.shape
    e = router_w.shape[1]
    slab_rows = d // 2 // LANES
    row = lambda i, s: (i, 0)
    per_seq = lambda i, s: (s[i], 0, 0)
    return pl.pallas_call(
        _norm2_kernel,
        out_shape=(jax.ShapeDtypeStruct((t, d), BF16),
                   jax.ShapeDtypeStruct((t * slab_rows, LANES), U32),
                   jax.ShapeDtypeStruct((t, e), F32)),
        grid_spec=pltpu.PrefetchScalarGridSpec(
            num_scalar_prefetch=1, grid=(t // tm,),
            in_specs=[pl.BlockSpec((tm, d), row),
                      pl.BlockSpec((1, d), lambda i, s: (0, 0)),
                      pl.BlockSpec((1, 1, d), per_seq),
                      pl.BlockSpec((1, 1, d), per_seq),
                      pl.BlockSpec((d, e), lambda i, s: (0, 0))],
            out_specs=(pl.BlockSpec((tm, d), row),
                       pl.BlockSpec((tm * slab_rows, LANES), row),
                       pl.BlockSpec((tm, e), row))),
        compiler_params=_cparams(("arbitrary",)),
        name="norm2_router",
    )(seq_of_tile, x, g.reshape(1, d), sc, sh, router_w)


def _proj_kernel(x_ref, w_ref, o_ref):
    o_ref[...] = _bdot(x_ref[...], w_ref[...]).astype(o_ref.dtype)


def _proj(x, w, tm, tn, out_dtype=BF16, name="proj"):
    t, k = x.shape
    n = w.shape[1]
    return pl.pallas_call(
        _proj_kernel,
        out_shape=jax.ShapeDtypeStruct((t, n), out_dtype),
        grid=(t // tm, n // tn),
        in_specs=[pl.BlockSpec((tm, k), lambda i, j: (i, 0)),
                  pl.BlockSpec((k, tn), lambda i, j: (0, j))],
        out_specs=pl.BlockSpec((tm, tn), lambda i, j: (i, j)),
        compiler_params=_cparams(("parallel", "arbitrary")),
        name=name,
    )(x, w)


def _glu_kernel(*refs, silu_first, n_a):
    if n_a is None:
        x_ref, wa_ref, wb_ref, o_ref = refs
        x = x_ref[...].astype(BF16)
    else:
        xa_ref, xb_ref, wa_ref, wb_ref, o_ref = refs
        x = _pick_source(pl.program_id(0), n_a, xa_ref, xb_ref).astype(BF16)
    a = _bdot(x, wa_ref[...])
    b = _bdot(x, wb_ref[...])
    if silu_first:
        y = a * _sigmoid(a) * b
    else:
        y = a * _sigmoid(b)
    o_ref[...] = y.astype(o_ref.dtype)


def _glu_proj(x, wa, wb, tm, tn, silu_first, name, x_tail=None):
    k = x.shape[1]
    n = wa.shape[1]
    if x_tail is None:
        t, n_a = x.shape[0], None
        x_args, x_specs = [x], [pl.BlockSpec((tm, k), lambda i, j: (i, 0))]
    else:
        t, n_a = x.shape[0] + x_tail.shape[0], x.shape[0] // tm
        x_args, x_specs = [x, x_tail], list(_two_source_specs((tm, k), n_a))
    return pl.pallas_call(
        functools.partial(_glu_kernel, silu_first=silu_first, n_a=n_a),
        out_shape=jax.ShapeDtypeStruct((t, n), BF16),
        grid=(t // tm, n // tn),
        in_specs=x_specs + [pl.BlockSpec((k, tn), lambda i, j: (0, j)),
                            pl.BlockSpec((k, tn), lambda i, j: (0, j))],
        out_specs=pl.BlockSpec((tm, tn), lambda i, j: (i, j)),
        compiler_params=_cparams(("parallel", "arbitrary")),
        name=name,
    )(*x_args, wa, wb)


def _merge_kernel(h_ref, a_ref, c_ref, wga_ref, wgb_ref, wa_ref, wc_ref, ba_ref, bb_ref, o_ref):
    h = h_ref[...]
    ga = _sigmoid(_bdot(h, wga_ref[...]) + ba_ref[...])
    gb = _sigmoid(_bdot(h, wgb_ref[...]) + bb_ref[...])
    ya = _bdot(a_ref[...], wa_ref[...])
    yb = _bdot(c_ref[...], wc_ref[...])
    o_ref[...] = (ga * ya + gb * yb).astype(o_ref.dtype)


def _merge(h, a, cv, wga, wgb, wa, wc, ba, bb, tm, tn):
    t, d = h.shape
    ka = a.shape[1]
    kc = cv.shape[1]
    row = lambda i, j: (i, 0)
    col = lambda i, j: (0, j)
    return pl.pallas_call(
        _merge_kernel,
        out_shape=jax.ShapeDtypeStruct((t, d), BF16),
        grid=(t // tm, d // tn),
        in_specs=[pl.BlockSpec((tm, d), row), pl.BlockSpec((tm, ka), row), pl.BlockSpec((tm, kc), row),
                  pl.BlockSpec((d, tn), col), pl.BlockSpec((d, tn), col),
                  pl.BlockSpec((ka, tn), col), pl.BlockSpec((kc, tn), col),
                  pl.BlockSpec((1, tn), col), pl.BlockSpec((1, tn), col)],
        out_specs=pl.BlockSpec((tm, tn), lambda i, j: (i, j)),
        compiler_params=_cparams(("parallel", "arbitrary")),
        name="merge",
    )(h, a, cv, wga, wgb, wa, wc, ba, bb)


def _outproj_kernel(seq_ref, m_ref, w_ref, xa_ref, xb_ref, g_ref, o_ref, *, n_a):
    del seq_ref
    x = _pick_source(pl.program_id(0), n_a, xa_ref, xb_ref)
    o_ref[...] = x + g_ref[0] * _bdot(m_ref[...], w_ref[...])


def _outproj(m, w, xa, xb, gate, seq_of_tile, tm, tn):
    t, k = m.shape
    d = w.shape[1]
    n_a = xa.shape[0] // tm
    return pl.pallas_call(
        functools.partial(_outproj_kernel, n_a=n_a),
        out_shape=jax.ShapeDtypeStruct((t, d), F32),
        grid_spec=pltpu.PrefetchScalarGridSpec(
            num_scalar_prefetch=1, grid=(t // tm, d // tn),
            in_specs=[pl.BlockSpec((tm, k), lambda i, j, s: (i, 0)),
                      pl.BlockSpec((k, tn), lambda i, j, s: (0, j)),
                      *_two_source_specs((tm, tn), n_a, n_inner=1),
                      pl.BlockSpec((1, 1, tn), lambda i, j, s: (s[i], 0, j))],
            out_specs=pl.BlockSpec((tm, tn), lambda i, j, s: (i, j))),
        compiler_params=_cparams(("parallel", "arbitrary")),
        name="outproj",
    )(seq_of_tile, m, w, xa, xb, gate)


def _lagmat_kernel(c_ref, b_ref, o_ref):
    for g in range(c_ref.shape[0]):
        o_ref[g] = jnp.dot(c_ref[g], b_ref[g], preferred_element_type=F32,
                           precision=lax.Precision.HIGHEST)


def _lagmat(cs, bs):
    g2, c, n2 = cs.shape
    w = bs.shape[2]
    gb = _tile(g2, 8)
    return pl.pallas_call(
        _lagmat_kernel,
        out_shape=jax.ShapeDtypeStruct((g2, c, w), F32),
        grid=(g2 // gb,),
        in_specs=[pl.BlockSpec((gb, c, n2), lambda i: (i, 0, 0)),
                  pl.BlockSpec((gb, n2, w), lambda i: (i, 0, 0))],
        out_specs=pl.BlockSpec((gb, c, w), lambda i: (i, 0, 0)),
        compiler_params=_cparams(("arbitrary",)),
        name="ssm_lagmat",
    )(cs, bs)


def _ssm_tables(a_re, a_im, log_dt, b_re, b_im, c_re, c_im, d_skip):
    tc, ch = SSM_CHUNK, SSM_GROUP_CH
    _, g, n = a_re.shape
    dt = jnp.exp(log_dt)[:, :, None]
    zr, zi = a_re * dt, a_im * dt
    ks = jnp.arange(tc + 1, dtype=F32)[:, None, None, None]
    mag = jnp.exp(ks * zr[None])
    pw_re, pw_im = mag * jnp.cos(ks * zi[None]), mag * jnp.sin(ks * zi[None])
    lb_re, lb_im = pw_re[1], pw_im[1]
    den = a_re * a_re + a_im * a_im
    q_re = ((lb_re - 1.0) * a_re + lb_im * a_im) / den
    q_im = (lb_im * a_re - (lb_re - 1.0) * a_im) / den
    bb_re = q_re[..., None] * b_re - q_im[..., None] * b_im
    bb_im = q_re[..., None] * b_im + q_im[..., None] * b_re
    p_re, p_im = pw_re[:tc, ..., None], pw_im[:tc, ..., None]
    bk_re = p_re * bb_re[None] - p_im * bb_im[None]
    bk_im = p_re * bb_im[None] + p_im * bb_re[None]

    bs = jnp.concatenate([bk_re, bk_im], axis=3)
    bs = bs.transpose(1, 2, 3, 0, 4).reshape(2 * g, 2 * n, tc * ch)
    cs = jnp.concatenate([c_re, -c_im], axis=2)
    cs = jnp.broadcast_to(cs[None], (2, g, ch, 2 * n)).reshape(2 * g, ch, 2 * n)
    m = _lagmat(cs, bs).reshape(2, g, ch, tc, ch)

    s_idx = np.arange(tc)[:, None]
    t_idx = np.arange(tc)[None, :]
    lag_f = np.clip(t_idx - s_idx, 0, tc - 1)
    lag_b = np.clip(s_idx - t_idx, 0, tc - 1)
    mf = m[0][:, :, lag_f, :] * jnp.asarray(t_idx >= s_idx, F32)[None, None, :, :, None]
    mb = m[1][:, :, lag_b, :] * jnp.asarray(s_idx >= t_idx, F32)[None, None, :, :, None]
    toep = (mf + mb).transpose(0, 2, 4, 3, 1).reshape(g, tc * ch, tc * ch)

    def rows_sc(x):
        return x.transpose(1, 0, 3, 2).reshape(g, tc * ch, n)
    pin = jnp.concatenate([rows_sc(bk_re[::-1, 0]), rows_sc(bk_re[:, 1]),
                           rows_sc(bk_im[::-1, 0]), rows_sc(bk_im[:, 1])], axis=2)

    def w_tab(pr, pi):
        wr = c_re[None] * pr[:, :, None, :] - c_im[None] * pi[:, :, None, :]
        wi = c_re[None] * pi[:, :, None, :] + c_im[None] * pr[:, :, None, :]
        to_rows = lambda x: x.transpose(1, 3, 0, 2).reshape(g, n, tc * ch)
        return to_rows(wr), to_rows(-wi)
    qf_re, qf_im = w_tab(pw_re[1:tc + 1, 0], pw_im[1:tc + 1, 0])
    qb_re, qb_im = w_tab(pw_re[tc:0:-1, 1], pw_im[tc:0:-1, 1])
    qout = jnp.concatenate([qf_re, qb_re, qf_im, qb_im], axis=1)

    a_step_re = jnp.concatenate([pw_re[tc, 0], pw_re[tc, 1]], axis=1)
    a_step_im = jnp.concatenate([pw_im[tc, 0], pw_im[tc, 1]], axis=1)
    d_tile = jnp.tile(d_skip.reshape(g, 1, ch), (1, tc, 1)).reshape(g, tc * ch)
    return (toep.astype(BF16), pin.astype(BF16), qout.astype(BF16),
            a_step_re, a_step_im, d_tile)


def _gelu_tanh(x):
    return 0.5 * x * (1.0 + jnp.tanh(math.sqrt(2.0 / math.pi) * (x + 0.044715 * (x * x * x))))


def _ssm_kernel(u_ref, t_ref, p_ref, q_ref, are_ref, aim_ref, d_ref, o_ref,
                ug_ref, yg_ref, sin_re, sin_im, hf_re, hf_im, hb_re, hb_im):
    gb, w, _ = t_ref.shape
    tc, ch = SSM_CHUNK, SSM_GROUP_CH
    nc = u_ref.shape[0] // tc
    half = w // 2
    n_col = w // LANES
    slots = LANES // ch
    slot = lax.broadcasted_iota(I32, (8, LANES), 1) // ch
    in_slot = [slot == k for k in range(slots)]

    def to_chunks(rc, _):
        xs = [u_ref[pl.ds(rc * (8 * tc) + t, 8, stride=tc), :] for t in range(tc)]
        cols = [[jnp.zeros((8, LANES), F32) for _ in range(n_col)] for _ in range(gb)]
        for col in range(n_col):
            for s in range(slots):
                v = jnp.zeros((8, LANES), F32)
                for ts in range(slots):
                    v = jnp.where(in_slot[(ts - s) % slots], xs[col * slots + ts], v)
                vr = v if s == 0 else pltpu.roll(v, s * ch, 1)
                for ts in range(slots):
                    g = (ts - s) % slots
                    cols[g][col] = jnp.where(in_slot[ts], vr, cols[g][col])
        for g in range(gb):
            ug_ref[g, pl.ds(rc * 8, 8), :] = jnp.concatenate(cols[g], axis=1)
        return 0

    lax.fori_loop(0, nc // 8, to_chunks, 0)

    for g in range(gb):
        s_in = _bdot(ug_ref[g].astype(BF16), p_ref[g])
        sin_re[pl.ds(g, nc, stride=gb), :] = s_in[:, :half]
        sin_im[pl.ds(g, nc, stride=gb), :] = s_in[:, half:]

    a_re = are_ref[...]
    a_im = aim_ref[...]
    is_fwd = lax.broadcasted_iota(I32, (gb, half), 1) < (half // 2)

    def step(j, carry):
        s_re, s_im = carry
        at_f = pl.ds(pl.multiple_of(j * gb, gb), gb)
        at_b = pl.ds(pl.multiple_of((nc - 1 - j) * gb, gb), gb)
        hf_re[at_f, :] = s_re
        hf_im[at_f, :] = s_im
        hb_re[at_b, :] = s_re
        hb_im[at_b, :] = s_im
        in_re = jnp.where(is_fwd, sin_re[at_f, :], sin_re[at_b, :])
        in_im = jnp.where(is_fwd, sin_im[at_f, :], sin_im[at_b, :])
        return (a_re * s_re - a_im * s_im + in_re, a_re * s_im + a_im * s_re + in_im)

    zero = jnp.zeros((gb, half), F32)
    lax.fori_loop(0, nc, step, (zero, zero))

    fwd_lane = lax.broadcasted_iota(I32, (nc, half), 1) < (half // 2)
    for g in range(gb):
        u = ug_ref[g]
        rows = pl.ds(g, nc, stride=gb)
        hin = jnp.concatenate([jnp.where(fwd_lane, hf_re[rows, :], hb_re[rows, :]),
                               jnp.where(fwd_lane, hf_im[rows, :], hb_im[rows, :])], axis=1)
        y = _bdot(u.astype(BF16), t_ref[g]) + _bdot(hin.astype(BF16), q_ref[g]) + d_ref[pl.ds(g, 1), :] * u
        yg_ref[g] = _gelu_tanh(y)

    def from_chunks(rc, _):
        for col in range(n_col):
            ys = [yg_ref[g, pl.ds(rc * 8, 8), pl.ds(col * LANES, LANES)] for g in range(gb)]
            z = [jnp.zeros((8, LANES), F32) for _ in range(slots)]
            for s in range(slots):
                v = jnp.zeros((8, LANES), F32)
                for g in range(gb):
                    v = jnp.where(in_slot[(g - s) % slots], ys[g], v)
                vr = v if s == 0 else pltpu.roll(v, s * ch, 1)
                for g in range(gb):
                    ts = (g - s) % slots
                    z[ts] = jnp.where(in_slot[g], vr, z[ts])
            for ts in range(slots):
                o_ref[pl.ds(rc * (8 * tc) + col * slots + ts, 8, stride=tc), :] = z[ts]
        return 0

    lax.fori_loop(0, nc // 8, from_chunks, 0)


def _ssm_seqs(u, tabs, row0, n_seq, seq_len):
    toep, pin, qout, a_re, a_im, d_tile = tabs
    _, wch = u.shape
    w = toep.shape[1]
    half = w // 2
    gb = LANES // SSM_GROUP_CH
    nc = seq_len // SSM_CHUNK
    assert nc % 8 == 0 and row0 % seq_len == 0 and wch % LANES == 0
    blk0 = row0 // seq_len
    tab_spec = pl.BlockSpec((gb, w, w), lambda i, b: (i, 0, 0))
    vec_spec = lambda width: pl.BlockSpec((gb, width), lambda i, b: (i, 0))
    return pl.pallas_call(
        _ssm_kernel,
        out_shape=jax.ShapeDtypeStruct((n_seq * seq_len, wch), F32),
        grid=(wch // LANES, n_seq),
        in_specs=[pl.BlockSpec((seq_len, LANES), lambda i, b: (blk0 + b, i)),
                  tab_spec, tab_spec, tab_spec,
                  vec_spec(half), vec_spec(half), vec_spec(w)],
        out_specs=pl.BlockSpec((seq_len, LANES), lambda i, b: (b, i)),
        scratch_shapes=[pltpu.VMEM((gb, nc, w), F32)] * 2 + [pltpu.VMEM((gb * nc, half), F32)] * 6,
        compiler_params=_cparams(("parallel", "arbitrary")),
        name="ssm",
    )(u, toep, pin, qout, a_re, a_im, d_tile)


def _ssm_branch(u, tabs, seqs):
    outs = []
    row0 = 0
    for n_seq, seq_len in seqs:
        outs.append(_ssm_seqs(u, tabs, row0, n_seq, seq_len))
        row0 += n_seq * seq_len
    return outs


def _conv_kernel(first_ref, last_ref, prev_ref, cur_ref, next_ref, w_ref, b_ref, g_ref, beta_ref,
                 o_ref, xs_ref, acc_ref, *, halo, row_chunk):
    i = pl.program_id(0)
    r, c = cur_ref.shape
    kw = w_ref.shape[0]
    half = kw // 2
    n_shift = xs_ref.shape[0]
    keep_prev = jnp.where(first_ref[i] == 0, 1.0, 0.0)
    keep_next = jnp.where(last_ref[i] == 0, 1.0, 0.0)
    xs_ref[0, pl.ds(0, halo), :] = prev_ref[...].astype(F32) * keep_prev
    xs_ref[0, pl.ds(halo, r), :] = cur_ref[...].astype(F32)
    xs_ref[0, pl.ds(halo + r, halo), :] = next_ref[...].astype(F32) * keep_next
    span = r + 2 * halo - n_shift
    for b in range(1, n_shift):
        xs_ref[b, pl.ds(0, span), :] = xs_ref[0, pl.ds(b, span), :]

    col_chunk = _tile(c, 4 * LANES)
    n_cc = c // col_chunk

    def chunk(it, _):
        r0 = pl.multiple_of((it // n_cc) * row_chunk, row_chunk)
        c0 = pl.multiple_of((it % n_cc) * col_chunk, col_chunk)
        cols = pl.ds(c0, col_chunk)
        acc = jnp.zeros((row_chunk, col_chunk), F32)
        for k in range(kw):
            off = halo - half + k
            b = off % n_shift
            acc = acc + w_ref[pl.ds(k, 1), cols] * xs_ref[b, pl.ds(r0 + (off - b), row_chunk), cols]
        acc_ref[pl.ds(r0, row_chunk), cols] = acc
        return 0

    lax.fori_loop(0, (r // row_chunk) * n_cc, chunk, 0)
    ln_rows = _tile(r, 16)

    def norm_rows(it, _):
        rows = pl.ds(pl.multiple_of(it * ln_rows, ln_rows), ln_rows)
        y = acc_ref[rows, :] + b_ref[...]
        mu = jnp.mean(y, axis=-1, keepdims=True)
        yc = y - mu
        var = jnp.mean(yc * yc, axis=-1, keepdims=True)
        z = yc * lax.rsqrt(var + LN_EPS) * g_ref[...] + beta_ref[...]
        o_ref[rows, :] = (z * _sigmoid(z)).astype(o_ref.dtype)
        return 0

    lax.fori_loop(0, r // ln_rows, norm_rows, 0, unroll=2)


def _conv_ln(ub, conv_w, conv_b, ln_g, ln_b, first_of_tile, last_of_tile, r):
    t, c = ub.shape
    kw = conv_w.shape[0]
    halo = 16
    assert kw // 2 <= halo and r % halo == 0
    nb = r // halo
    n_halo_blocks = t // halo
    row_chunk = _tile(r, 32)
    vec = lambda i, f, l: (0, 0)
    return pl.pallas_call(
        functools.partial(_conv_kernel, halo=halo, row_chunk=row_chunk),
        out_shape=jax.ShapeDtypeStruct((t, c), BF16),
        grid_spec=pltpu.PrefetchScalarGridSpec(
            num_scalar_prefetch=2, grid=(t // r,),
            in_specs=[pl.BlockSpec((halo, c), lambda i, f, l: (jnp.maximum(i * nb - 1, 0), 0)),
                      pl.BlockSpec((r, c), lambda i, f, l: (i, 0)),
                      pl.BlockSpec((halo, c), lambda i, f, l: (jnp.minimum((i + 1) * nb, n_halo_blocks - 1), 0)),
                      pl.BlockSpec((kw, c), vec), pl.BlockSpec((1, c), vec),
                      pl.BlockSpec((1, c), vec), pl.BlockSpec((1, c), vec)],
            out_specs=pl.BlockSpec((r, c), lambda i, f, l: (i, 0)),
            scratch_shapes=[pltpu.VMEM((8, r + 2 * halo, c), F32), pltpu.VMEM((r, c), F32)]),
        compiler_params=_cparams(("arbitrary",)),
        name="conv_ln",
    )(first_of_tile, last_of_tile, ub, ub, ub, conv_w, conv_b.reshape(1, c),
      ln_g.reshape(1, c), ln_b.reshape(1, c))


def _seg_allreduce(x, lane, seg, op):
    n = x.shape[-1]
    s = 1
    while s < seg:
        up = pltpu.roll(x, n - s, 1)
        dn = pltpu.roll(x, s, 1)
        x = op(x, jnp.where((lane & s) == 0, up, dn))
        s *= 2
    return x


def _route_kernel(lg_ref, bias_ref, idx_ref, wt_ref, rank_ref, cnt_ref, carry_ref, *, n_groups, topk_groups, top_k):
    i = pl.program_id(0)
    tm, e = lg_ref.shape
    seg = e // n_groups
    lane = lax.broadcasted_iota(I32, (tm, e), 1)
    lane_f = lane.astype(F32)
    neg = jnp.float32(-jnp.inf)

    scores = _sigmoid(lg_ref[...])
    choice = scores + bias_ref[...]

    m1 = _seg_allreduce(choice, lane, seg, jnp.maximum)
    first = _seg_allreduce(jnp.where(choice == m1, lane_f, float(e)), lane, seg, jnp.minimum)
    m2 = _seg_allreduce(jnp.where(lane_f == first, neg, choice), lane, seg, jnp.maximum)
    gscore = m1 + m2

    beaten = jnp.zeros((tm, e), F32)
    for dshift in range(1, n_groups):
        other = pltpu.roll(gscore, dshift * seg, 1)
        other_is_lower = lane >= dshift * seg
        beaten = beaten + jnp.where(other_is_lower, jnp.where(other >= gscore, 1.0, 0.0),
                                    jnp.where(other > gscore, 1.0, 0.0))
    masked = jnp.where(beaten < topk_groups, choice, neg)

    idx_out = jnp.zeros((tm, e), F32)
    sc_out = jnp.zeros((tm, e), F32)
    sel = jnp.zeros((tm, e), F32)
    picks = []
    for k in range(top_k):
        m = jnp.max(masked, axis=1, keepdims=True)
        pick = jnp.min(jnp.where(masked == m, lane_f, float(e)), axis=1, keepdims=True)
        hit = lane_f == pick
        s_k = jnp.sum(jnp.where(hit, scores, 0.0), axis=1, keepdims=True)
        idx_out = jnp.where(lane == k, pick, idx_out)
        sc_out = jnp.where(lane == k, s_k, sc_out)
        sel = jnp.where(hit, 1.0, sel)
        masked = jnp.where(hit, neg, masked)
        picks.append(hit)
    denom = jnp.sum(sc_out, axis=1, keepdims=True)
    idx_ref[...] = idx_out.astype(I32)
    wt_ref[...] = sc_out / denom * ROUTED_SCALE

    @pl.when(i == 0)
    def _():
        carry_ref[...] = jnp.zeros_like(carry_ref)

    rows = lax.broadcasted_iota(I32, (tm, tm), 0)
    cols = lax.broadcasted_iota(I32, (tm, tm), 1)
    tri = jnp.where(cols < rows, 1.0, 0.0).astype(BF16)
    rank = _bdot(tri, sel.astype(BF16)) + carry_ref[...]
    rank_out = jnp.zeros((tm, e), F32)
    for k in range(top_k):
        r_k = jnp.sum(jnp.where(picks[k], rank, 0.0), axis=1, keepdims=True)
        rank_out = jnp.where(lane == k, r_k, rank_out)
    rank_ref[...] = rank_out.astype(I32)
    carry_ref[...] = carry_ref[...] + jnp.sum(sel, axis=0, keepdims=True)
    cnt_ref[...] = carry_ref[...].astype(I32)


def _route(logits, bias, tm):
    t, e = logits.shape
    row = lambda i: (i, 0)
    fixed = lambda i: (0, 0)
    return pl.pallas_call(
        functools.partial(_route_kernel, n_groups=N_ROUTE_GROUPS, topk_groups=TOPK_ROUTE_GROUPS, top_k=TOP_K),
        out_shape=(jax.ShapeDtypeStruct((t, e), I32), jax.ShapeDtypeStruct((t, e), F32),
                   jax.ShapeDtypeStruct((t, e), I32), jax.ShapeDtypeStruct((1, e), I32)),
        grid=(t // tm,),
        in_specs=[pl.BlockSpec((tm, e), row), pl.BlockSpec((1, e), fixed)],
        out_specs=(pl.BlockSpec((tm, e), row), pl.BlockSpec((tm, e), row),
                   pl.BlockSpec((tm, e), row), pl.BlockSpec((1, e), fixed)),
        scratch_shapes=[pltpu.VMEM((1, e), F32)],
        compiler_params=_cparams(("arbitrary",)),
        name="route",
    )(logits, bias.reshape(1, e))


def _dest_kernel(idx_ref, rank_ref, start_ref, o_ref, *, top_k):
    tm, e = idx_ref.shape
    lane = lax.broadcasted_iota(I32, (tm, e), 1)
    idx = idx_ref[...]
    start = start_ref[...].astype(F32)
    out = jnp.zeros((tm, e), F32)
    for k in range(top_k):
        s_k = jnp.sum(jnp.where(lane == idx[:, k:k + 1], start, 0.0), axis=1, keepdims=True)
        out = jnp.where(lane == k, s_k, out)
    o_ref[...] = out.astype(I32) + rank_ref[...]


def _dest_rows(top_idx, rank, pad_start, tm):
    t, e = top_idx.shape
    row = lambda i: (i, 0)
    return pl.pallas_call(
        functools.partial(_dest_kernel, top_k=TOP_K),
        out_shape=jax.ShapeDtypeStruct((t, e), I32),
        grid=(t // tm,),
        in_specs=[pl.BlockSpec((tm, e), row), pl.BlockSpec((tm, e), row), pl.BlockSpec((1, e), lambda i: (0, 0))],
        out_specs=pl.BlockSpec((tm, e), row),
        compiler_params=_cparams(("arbitrary",)),
        name="dest_rows",
    )(top_idx, rank, pad_start.reshape(1, e))


def _expert_kernel(blk_e_ref, nblk_ref, wslot_ref, wfirst_ref, wnext_ref, idx_hbm, x_hbm, wg_hbm, wu_hbm, wd_hbm,
                   ys_hbm, idx_smem, xb0, xb1, ob0, ob1, wg_buf, wu_buf, wd_buf, sem_idx, sem_g, sem_s, sem_w):
    b = pl.program_id(0)
    n = nblk_ref[0]
    kh = wg_buf.shape[1] // 2
    ns = kh // LANES

    def weight_copies(e, s):
        return [pltpu.make_async_copy(hbm.at[e], buf.at[s], sem_w.at[s])
                for hbm, buf in ((wg_hbm, wg_buf), (wu_hbm, wu_buf), (wd_hbm, wd_buf))]
    pitch = ns + 1
    r = xb0.shape[0] // pitch
    xbufs = (xb0, xb1)
    obufs = (ob0, ob1)
    dump0 = ys_hbm.shape[0] // pitch - 2 * r

    def idx_copy(blk, s):
        return pltpu.make_async_copy(idx_hbm.at[blk], idx_smem.at[s], sem_idx.at[s])

    def x_row(row):
        return x_hbm.at[pl.ds(pl.multiple_of(row * ns, ns), ns)]

    def y_row(row):
        return ys_hbm.at[pl.ds(row * pitch, pitch)]

    def gather_start(s):
        for i in range(r):
            pltpu.make_async_copy(x_row(idx_smem[s, 0, i]), xbufs[s].at[pl.ds(i * pitch, ns)], sem_g.at[s]).start()

    def gather_wait(s):
        for i in range(r):
            pltpu.make_async_copy(x_row(0), xbufs[s].at[pl.ds(i * pitch, ns)], sem_g.at[s]).wait()

    def scatter_start(s):
        for i in range(r):
            pltpu.make_async_copy(obufs[s].at[pl.ds(i * pitch, pitch)], y_row(idx_smem[s, 1, i]), sem_s.at[s]).start()

    def scatter_wait(s):
        for i in range(r):
            pltpu.make_async_copy(obufs[s].at[pl.ds(i * pitch, pitch)], y_row(0), sem_s.at[s]).wait()

    @pl.when(b == 0)
    def _():
        ob0[...] = jnp.zeros_like(ob0)
        ob1[...] = jnp.zeros_like(ob1)
        for h in range(2):
            fill = pltpu.make_async_copy(ob1, ys_hbm.at[pl.ds((dump0 + h * r) * pitch, r * pitch)], sem_s.at[1])
            fill.start()
            fill.wait()
        for c in weight_copies(blk_e_ref[0], 0):
            c.start()
        first = idx_copy(0, 0)
        first.start()
        first.wait()
        gather_start(0)

        @pl.when(n > 1)
        def _():
            idx_copy(1, 1).start()

    def step(cur):
        nxt = 1 - cur
        gather_wait(cur)

        @pl.when(b + 1 < n)
        def _():
            idx_copy(b + 1, nxt).wait()
            gather_start(nxt)

        @pl.when(b >= 2)
        def _():
            scatter_wait(cur)

        ws = wslot_ref[b]

        @pl.when(wfirst_ref[b] == 1)
        def _():
            for c in weight_copies(blk_e_ref[b], ws):
                c.wait()

            @pl.when(wnext_ref[b] >= 0)
            def _():
                for c in weight_copies(wnext_ref[b], 1 - ws):
                    c.start()

        lo, hi = _unpack_bf16_pair(_load_row_slabs(xbufs[cur], r, ns, pitch))
        lo = lo.astype(BF16)
        hi = hi.astype(BF16)
        gate = _bdot(lo, wg_buf[ws, pl.ds(0, kh), :]) + _bdot(hi, wg_buf[ws, pl.ds(kh, kh), :])
        up = _bdot(lo, wu_buf[ws, pl.ds(0, kh), :]) + _bdot(hi, wu_buf[ws, pl.ds(kh, kh), :])
        act = (gate * _sigmoid(gate) * up).astype(BF16)
        out = _bdot(act, wd_buf[ws])
        _store_row_slabs(obufs[cur], _pack_bf16_pair(out[:, :kh], out[:, kh:]), pitch)
        scatter_start(cur)

        @pl.when(b + 2 < n)
        def _():
            idx_copy(b + 2, cur).start()

        @pl.when(b == n - 1)
        def _():
            scatter_wait(cur)

            @pl.when(b >= 1)
            def _():
                scatter_wait(nxt)

    for parity in range(2):
        @pl.when((b < n) & ((b & 1) == parity))
        def _(parity=parity):
            step(parity)


def _experts(blk_e, nblk, row_idx, xp, wg, wu, wd, n_out_rows):
    n_blocks, _, r = row_idx.shape
    n_e, d, de = wg.shape
    ns = d // 2 // LANES
    pos = jnp.arange(n_blocks, dtype=I32)
    used = pos < nblk[0]
    first = used & jnp.concatenate([jnp.ones((1,), bool), blk_e[1:] != blk_e[:-1]])
    wslot = ((jnp.cumsum(first.astype(I32)) - 1) % 2).astype(I32)
    first_pos = jnp.where(first, pos, n_blocks)
    next_first = jnp.concatenate([lax.cummin(first_pos[::-1])[::-1][1:], jnp.full((1,), n_blocks, I32)])
    wnext = jnp.where(next_first < n_blocks, blk_e[jnp.minimum(next_first, n_blocks - 1)], -1).astype(I32)
    return pl.pallas_call(
        _expert_kernel,
        out_shape=jax.ShapeDtypeStruct(((n_out_rows + 2 * r) * (ns + 1), LANES), U32),
        grid_spec=pltpu.PrefetchScalarGridSpec(
            num_scalar_prefetch=5, grid=(n_blocks,),
            in_specs=[pl.BlockSpec(memory_space=pl.ANY)] * 5,
            out_specs=pl.BlockSpec(memory_space=pl.ANY),
            scratch_shapes=[pltpu.SMEM((2, 2, r), I32)] + [pltpu.VMEM((r * (ns + 1), LANES), U32)] * 4
                           + [pltpu.VMEM((2, d, de), BF16)] * 2 + [pltpu.VMEM((2, de, d), BF16)]
                           + [pltpu.SemaphoreType.DMA((2,))] * 4),
        compiler_params=_cparams(("arbitrary",)),
        name="experts",
    )(blk_e, nblk, wslot, first.astype(I32), wnext, row_idx, xp, wg, wu, wd)


def _combine_kernel(seq_ref, *refs, top_k, n_a):
    del seq_ref
    y_refs = refs[:top_k]
    wt_ref, act_ref, wsd_ref, x_ref, g2_ref, fn_ref, oa_ref, ob_ref, routed_ref = refs[top_k:]
    i = pl.program_id(0)
    tm, d = x_ref.shape
    kh = d // 2
    ns = kh // LANES
    wt = wt_ref[...]
    w_cols = [wt[:, k:k + 1] for k in range(top_k)]
    for j in range(ns):
        r_lo = jnp.zeros((tm, LANES), F32)
        r_hi = jnp.zeros((tm, LANES), F32)
        for k in range(top_k):
            lo, hi = _unpack_bf16_pair(y_refs[k][pl.ds(j, tm, stride=ns + 1), :])
            r_lo = r_lo + w_cols[k] * lo
            r_hi = r_hi + w_cols[k] * hi
        routed_ref[:, pl.ds(j * LANES, LANES)] = r_lo
        routed_ref[:, pl.ds(kh + j * LANES, LANES)] = r_hi
    shared = _bdot(act_ref[...], wsd_ref[...])
    y = x_ref[...] + g2_ref[0] * (shared + routed_ref[...])
    ms = jnp.mean(y * y, axis=-1, keepdims=True)
    out = y * lax.rsqrt(ms + RMS_EPS) * fn_ref[...]

    @pl.when(i < n_a)
    def _():
        oa_ref[...] = out

    @pl.when(i >= n_a)
    def _():
        ob_ref[...] = out


def _combine(ys, wt, act, wsd, x1, g2, final_norm, seq_of_tile, tm, rows_a):
    t, d = x1.shape
    n_a = rows_a // tm
    e = wt.shape[1]
    ds_ = act.shape[1]
    ns = d // 2 // LANES
    nt = t // tm
    row = lambda i, s: (i, 0)
    fixed = lambda i, s: (0, 0)
    y_specs = [pl.BlockSpec((tm * (ns + 1), LANES), functools.partial(lambda i, s, k: (k * nt + i, 0), k=k))
               for k in range(TOP_K)]
    return pl.pallas_call(
        functools.partial(_combine_kernel, top_k=TOP_K, n_a=n_a),
        out_shape=(jax.ShapeDtypeStruct((rows_a, d), F32), jax.ShapeDtypeStruct((t - rows_a, d), F32)),
        grid_spec=pltpu.PrefetchScalarGridSpec(
            num_scalar_prefetch=1, grid=(nt,),
            in_specs=y_specs + [pl.BlockSpec((tm, e), row),
                                pl.BlockSpec((tm, ds_), row),
                                pl.BlockSpec((ds_, d), fixed),
                                pl.BlockSpec((tm, d), row),
                                pl.BlockSpec((1, 1, d), lambda i, s: (s[i], 0, 0)),
                                pl.BlockSpec((1, d), fixed)],
            out_specs=_two_source_specs((tm, d), n_a),
            scratch_shapes=[pltpu.VMEM((tm, d), F32)]),
        compiler_params=_cparams(("arbitrary",)),
        name="combine",
    )(seq_of_tile, *([ys] * TOP_K), wt, act, wsd, x1, g2, final_norm.reshape(1, d))


def _seq_tables(seqs, tile):
    sid, first, last = [], [], []
    s = 0
    for n_seq, seq_len in seqs:
        per = seq_len // tile
        for _ in range(n_seq):
            sid += [s] * per
            first += [1] + [0] * (per - 1)
            last += [0] * (per - 1) + [1]
            s += 1
    as_i32 = lambda v: jnp.asarray(np.asarray(v, np.int32))
    return as_i32(sid), as_i32(first), as_i32(last)


def _layer(xa, xb, c_all, seqs, p):
    d = xa.shape[1]
    t = xa.shape[0] + xb.shape[0]
    w_ssm = p["ssm_d"].shape[0]
    w_conv = p["conv_b"].shape[0]
    n_e = p["router_w"].shape[1]
    min_seq = min(sl for _, sl in seqs)

    tm = _tile(min_seq, 1024)
    tn_row = _tile(min_seq, 256)
    seq_mm, _, _ = _seq_tables(seqs, tm)
    seq_nr, _, _ = _seq_tables(seqs, tn_row)

    n_seq_total = c_all.shape[0]
    ada = _ada(c_all, p["w_ada"], p["b_ada"])
    sh1, sc1, g1, sh2, sc2, g2 = [a.reshape(n_seq_total, 1, d) for a in jnp.split(ada, 6, axis=1)]

    bf = lambda w: w.astype(BF16)
    w_in = p["w_in"]
    h = _norm1(xa, xb, p["norm_mix"], sc1, sh1, seq_nr, tn_row)

    u_ssm = _proj(h, bf(w_in[:, :w_ssm]), tm, _tile(w_ssm, 1024), out_dtype=F32, name="in_ssm")
    tabs = _ssm_tables(p["ssm_a_re"], p["ssm_a_im"], p["ssm_log_dt"], p["ssm_b_re"], p["ssm_b_im"],
                       p["ssm_c_re"], p["ssm_c_im"], p["ssm_d"])
    ya_a, ya_b = _ssm_branch(u_ssm, tabs, seqs)
    wgl = p["w_ssm_glu"]
    a_act = _glu_proj(ya_a, bf(wgl[:, :w_ssm]), bf(wgl[:, w_ssm:]), _tile(min_seq, 512), _tile(w_ssm, 1024), False,
                      "ssm_glu", x_tail=ya_b)

    ub = _glu_proj(h, bf(w_in[:, w_ssm:w_ssm + w_conv]), bf(w_in[:, w_ssm + w_conv:]),
                   tm, _tile(w_conv, 512), False, "in_conv_glu")
    r_conv = _tile(min_seq, 256)
    _, first_c, last_c = _seq_tables(seqs, r_conv)
    cv = _conv_ln(ub, p["conv_w"], p["conv_b"], p["conv_norm_g"], p["conv_norm_b"], first_c, last_c, r_conv)

    wmg = p["w_merge_gate"]
    bmg = p["b_merge_gate"]
    tm_merge = _tile(min_seq, 512)
    m = _merge(h, a_act, cv, bf(wmg[:, :d]), bf(wmg[:, d:]), bf(p["w_branch_ssm"]), bf(p["w_branch_conv"]),
               bmg[:d].reshape(1, d), bmg[d:].reshape(1, d), tm_merge, _tile(d, 512))
    x1 = _outproj(m, bf(p["w_out"]), xa, xb, g1, seq_mm, tm, _tile(d, 512))

    h2, h2p, logits = _norm2(x1, p["norm_ffn"], sc2, sh2, p["router_w"], seq_nr, tn_row)
    tm_route = _tile(t, 256)
    top_idx, top_w, rank, counts = _route(logits, p["router_bias"], tm_route)

    r_blk = 256 if t * TOP_K >= 256 * n_e else 8
    counts = counts[0]
    padded = (counts + r_blk - 1) // r_blk * r_blk
    pad_end = jnp.cumsum(padded)
    pad_start = pad_end - padded
    n_blocks = -(-(t * TOP_K) // r_blk) + n_e
    dest = _dest_rows(top_idx, rank, pad_start, tm_route)[:, :TOP_K]
    pos = jnp.arange(n_blocks * r_blk, dtype=I32)
    dump = TOP_K * t + ((pos // r_blk) % 2) * r_blk + pos % r_blk
    out_rows = jnp.arange(TOP_K, dtype=I32)[None, :] * t + jnp.arange(t, dtype=I32)[:, None]
    row_dst = dump.at[dest.reshape(-1)].set(out_rows.reshape(-1))
    row_src = jnp.where(row_dst < TOP_K * t, row_dst % t, 0)
    row_idx = jnp.stack([row_src.reshape(n_blocks, r_blk), row_dst.reshape(n_blocks, r_blk)], axis=1)
    blk_start = jnp.arange(n_blocks, dtype=I32) * r_blk
    blk_e = jnp.minimum(jnp.searchsorted(pad_end, blk_start, side="right"), n_e - 1).astype(I32)
    nblk = (pad_end[-1] // r_blk).astype(I32).reshape(1)

    ys = _experts(blk_e, nblk, row_idx, h2p, bf(p["exp_w_gate"]), bf(p["exp_w_up"]), bf(p["exp_w_down"]),
                  TOP_K * t)

    act = _glu_proj(h2, bf(p["shared_w_gate"]), bf(p["shared_w_up"]), tm, p["shared_w_gate"].shape[1],
                    True, "shared_act")
    tm_c = _tile(min_seq, 128)
    seq_c, _, _ = _seq_tables(seqs, tm_c)
    return _combine(ys, top_w, act, bf(p["shared_w_down"]), x1, g2, p["final_norm"], seq_c, tm_c, xa.shape[0])


def kernel(x_prompt, x_sample, c_prompt, c_sample, w_ada, b_ada, norm_mix, w_in, ssm_a_re, ssm_a_im, ssm_log_dt, ssm_b_re, ssm_b_im, ssm_c_re, ssm_c_im, ssm_d, w_ssm_glu, w_branch_ssm, conv_w, conv_b, conv_norm_g, conv_norm_b, w_branch_conv, w_merge_gate, b_merge_gate, w_out, norm_ffn, router_w, router_bias, exp_w_gate, exp_w_up, exp_w_down, shared_w_gate, shared_w_up, shared_w_down, final_norm):
    assert w_ada.shape[0] == 1, "single-layer trunk"
    bp, lp, d = x_prompt.shape
    bs, ls, _ = x_sample.shape
    seqs = [(bp, lp), (bs, ls)]
    n_seq = bp + bs
    c_all = jnp.concatenate([c_prompt, c_sample, jnp.zeros((-n_seq % 8, d), F32)], axis=0)
    params = dict(
        w_ada=w_ada[0], b_ada=b_ada[0], norm_mix=norm_mix[0], w_in=w_in[0],
        ssm_a_re=ssm_a_re[0], ssm_a_im=ssm_a_im[0], ssm_log_dt=ssm_log_dt[0],
        ssm_b_re=ssm_b_re[0], ssm_b_im=ssm_b_im[0], ssm_c_re=ssm_c_re[0], ssm_c_im=ssm_c_im[0],
        ssm_d=ssm_d[0], w_ssm_glu=w_ssm_glu[0], w_branch_ssm=w_branch_ssm[0],
        conv_w=conv_w[0], conv_b=conv_b[0], conv_norm_g=conv_norm_g[0], conv_norm_b=conv_norm_b[0],
        w_branch_conv=w_branch_conv[0], w_merge_gate=w_merge_gate[0], b_merge_gate=b_merge_gate[0],
        w_out=w_out[0], norm_ffn=norm_ffn[0], router_w=router_w[0], router_bias=router_bias[0],
        exp_w_gate=exp_w_gate[0], exp_w_up=exp_w_up[0], exp_w_down=exp_w_down[0],
        shared_w_gate=shared_w_gate[0], shared_w_up=shared_w_up[0], shared_w_down=shared_w_down[0],
        final_norm=final_norm)
    y_a, y_b = _layer(x_prompt.reshape(bp * lp, d), x_sample.reshape(bs * ls, d), c_all, seqs, params)
    return (y_a.reshape(bp, lp, d), y_b.reshape(bs, ls, d))
```
